```python
import jax, jax.numpy as jnp
from jax import lax
import numpy as np

D_MODEL = 1024
BATCH = 16
SEQ = 4096
DEPTH = 1
DEC_BATCH = 4
DEC_SEQ = 8192
PAST_LEN = 128

N_MEM = 256
MIX_WIDTH = D_MODEL
RET_WIDTH = MIX_WIDTH // 2
RET_HEADS = 4
RET_HEAD_DIM = RET_WIDTH // RET_HEADS
RET_CHUNK = 128
SGU_WIDTH = MIX_WIDTH - RET_WIDTH
SGU_GROUPS = 4
SGU_GROUP_DIM = SGU_WIDTH // SGU_GROUPS
SGU_CHUNK = 128
XA_HEADS = 4
XA_HEAD_DIM = D_MODEL // XA_HEADS
D_FF = ((-(-8 * D_MODEL // 3)) + 255) // 256 * 256
W_IN_COLS = 4 * RET_WIDTH + 2 * SGU_WIDTH
ROPE_BASE = 10000.0
EPS = 1e-6
N_NORMS = 7
NORM_PRE_MIX, NORM_POST_MIX, NORM_PRE_XA, NORM_POST_XA, NORM_MEM, NORM_PRE_FFN, NORM_POST_FFN = range(N_NORMS)

kernel_name = "hybrid_retention_sgu_encoder"


def rms_norm(x, w):
    xf = x.astype(jnp.float32)
    y = xf * lax.rsqrt(jnp.mean(xf * xf, axis=-1, keepdims=True) + EPS)
    return (y * w.astype(jnp.float32)).astype(x.dtype)


def layer_norm_nobias(x, w):
    xf = x.astype(jnp.float32)
    mu = jnp.mean(xf, axis=-1, keepdims=True)
    var = jnp.mean(jnp.square(xf - mu), axis=-1, keepdims=True)
    return ((xf - mu) * lax.rsqrt(var + EPS) * w.astype(jnp.float32)).astype(x.dtype)


def rotary(x):
    s, dh = x.shape[1], x.shape[3]
    half = dh // 2
    inv = ROPE_BASE ** (-jnp.arange(half, dtype=jnp.float32) / half)
    ang = jnp.arange(s, dtype=jnp.float32)[:, None] * inv[None, :]
    cos = jnp.cos(ang)[None, :, None, :]
    sin = jnp.sin(ang)[None, :, None, :]
    xf = x.astype(jnp.float32)
    x1, x2 = xf[..., :half], xf[..., half:]
    return jnp.concatenate([x1 * cos - x2 * sin, x1 * sin + x2 * cos], axis=-1)


def decay_scan(contrib, decay_chunk, reverse):
    def step(state, c):
        return decay_chunk[None, :, None, None] * state + c, state
    _, states = lax.scan(step, jnp.zeros_like(contrib[0]), contrib, reverse=reverse)
    return states


def bidir_retention(q, k, v, log_gamma):
    b, s, h, d = q.shape
    c = RET_CHUNK
    nc = s // c
    lg = log_gamma.astype(jnp.float32)
    lf, lb = lg[0], lg[1]
    idx = jnp.arange(c, dtype=jnp.float32)
    qc = q.reshape(b, nc, c, h, d)
    kc = k.reshape(b, nc, c, h, d)
    vc = v.reshape(b, nc, c, h, d)
    dist = idx[:, None] - idx[None, :]
    adist = jnp.abs(dist)
    d_intra = jnp.where(dist[None] >= 0, jnp.exp(lf[:, None, None] * adist[None]),
                        jnp.exp(lb[:, None, None] * adist[None]))
    scores = jnp.einsum('bcihd,bcjhd->bchij', qc, kc) * d_intra[None, None]
    y = jnp.einsum('bchij,bcjhe->bcihe', scores, vc)
    kf = kc * jnp.exp(lf[None, :] * (c - 1 - idx)[:, None])[None, None, :, :, None]
    w_f = jnp.einsum('bcjhd,bcjhe->cbhde', kf, vc)
    r_f = decay_scan(w_f, jnp.exp(lf * c), reverse=False)
    qf = qc * jnp.exp(lf[None, :] * (idx + 1.0)[:, None])[None, None, :, :, None]
    y = y + jnp.einsum('bcihd,cbhde->bcihe', qf, r_f)
    kb = kc * jnp.exp(lb[None, :] * idx[:, None])[None, None, :, :, None]
    w_b = jnp.einsum('bcjhd,bcjhe->cbhde', kb, vc)
    r_b = decay_scan(w_b, jnp.exp(lb * c), reverse=True)
    qb = qc * jnp.exp(lb[None, :] * (c - idx)[:, None])[None, None, :, :, None]
    y = y + jnp.einsum('bcihd,cbhde->bcihe', qb, r_b)
    return y.reshape(b, s, h, d)


def token_mixer(h, w_in, ret_log_gamma, ret_gn_w, sgu_norm_w, sgu_w, sgu_b, w_out):
    b, s, _ = h.shape
    z = h @ w_in
    q, k, v, g, u, vs = jnp.split(z, [RET_WIDTH, 2 * RET_WIDTH, 3 * RET_WIDTH, 4 * RET_WIDTH,
                                      4 * RET_WIDTH + SGU_WIDTH], axis=-1)
    q = rotary(q.reshape(b, s, RET_HEADS, RET_HEAD_DIM))
    k = rotary(k.reshape(b, s, RET_HEADS, RET_HEAD_DIM)) * (RET_HEAD_DIM ** -0.5)
    vh = v.reshape(b, s, RET_HEADS, RET_HEAD_DIM).astype(jnp.float32)
    y = bidir_retention(q, k, vh, ret_log_gamma)
    mu = jnp.mean(y, axis=-1, keepdims=True)
    var = jnp.mean(jnp.square(y - mu), axis=-1, keepdims=True)
    y = ((y - mu) * lax.rsqrt(var + EPS)).reshape(b, s, RET_WIDTH) * ret_gn_w.astype(jnp.float32)
    ret_out = (jax.nn.silu(g.astype(jnp.float32)) * y).astype(h.dtype)
    u = jax.nn.gelu(u)
    vs = layer_norm_nobias(jax.nn.gelu(vs), sgu_norm_w)
    nc = s // SGU_CHUNK
    vs = vs.reshape(b, nc, SGU_CHUNK, SGU_GROUPS, SGU_GROUP_DIM)
    sp = jnp.einsum('gpq,bcqgd->bcpgd', sgu_w, vs) + jnp.transpose(sgu_b)[None, None, :, :, None]
    sgu_out = u * sp.reshape(b, s, SGU_WIDTH)
    mixed = jnp.concatenate([ret_out, sgu_out.astype(h.dtype)], axis=-1)
    return mixed @ w_out


def memory_cross_attention(h, mem_n, wq, wkv, wo):
    b, s, _ = h.shape
    m = mem_n.shape[1]
    q = (h @ wq).reshape(b, s, XA_HEADS, XA_HEAD_DIM)
    kv = mem_n @ wkv
    k = kv[..., :D_MODEL].reshape(b, m, XA_HEADS, XA_HEAD_DIM)
    v = kv[..., D_MODEL:].reshape(b, m, XA_HEADS, XA_HEAD_DIM)
    sc = jnp.einsum('bshd,bmhd->bhsm', q, k).astype(jnp.float32) * (XA_HEAD_DIM ** -0.5)
    p = jax.nn.softmax(sc, axis=-1).astype(v.dtype)
    o = jnp.einsum('bhsm,bmhd->bshd', p, v).reshape(b, s, D_MODEL)
    return o @ wo


def swiglu(h, w_gu, w_down):
    gu = h @ w_gu
    gate, up = gu[..., :D_FF], gu[..., D_FF:]
    return (jax.nn.silu(gate) * up) @ w_down


def encoder_layer(x, mem, nw, w_in, ret_log_gamma, ret_gn_w, sgu_norm_w, sgu_w, sgu_b, w_out,
                  xa_wq, xa_wkv, xa_wo, ffn_w_gu, ffn_w_down):
    mix = token_mixer(rms_norm(x, nw[NORM_PRE_MIX]), w_in, ret_log_gamma, ret_gn_w,
                      sgu_norm_w, sgu_w, sgu_b, w_out)
    x = x + rms_norm(mix, nw[NORM_POST_MIX])
    mem_n = rms_norm(mem, nw[NORM_MEM])
    xa = memory_cross_attention(rms_norm(x, nw[NORM_PRE_XA]), mem_n, xa_wq, xa_wkv, xa_wo)
    x = x + rms_norm(xa, nw[NORM_POST_XA])
    ff = swiglu(rms_norm(x, nw[NORM_PRE_FFN]), ffn_w_gu, ffn_w_down)
    return x + rms_norm(ff, nw[NORM_POST_FFN])


def setup_inputs(seed: int = 0) -> dict:
    key = jax.random.key(seed)
    ks = jax.random.split(key, 20)
    nrm = jax.random.normal
    f32 = jnp.float32
    base_lg = jnp.log(1.0 - 2.0 ** (-5.0 - jnp.arange(RET_HEADS, dtype=f32)))
    return {
        "x_prompt": nrm(ks[0], (BATCH, SEQ, D_MODEL), f32),
        "x_sample": nrm(ks[1], (DEC_BATCH, DEC_SEQ, D_MODEL), f32),
        "mem_prompt": nrm(ks[2], (BATCH, N_MEM, D_MODEL), f32),
        "mem_sample": nrm(ks[3], (DEC_BATCH, N_MEM, D_MODEL), f32),
        "norm_w": 1.0 + 0.05 * nrm(ks[4], (DEPTH, N_NORMS, D_MODEL), f32),
        "w_in": nrm(ks[5], (DEPTH, D_MODEL, W_IN_COLS), f32) * D_MODEL ** -0.5,
        "ret_log_gamma": base_lg[None, None, :] * jnp.exp(0.1 * nrm(ks[6], (DEPTH, 2, RET_HEADS), f32)),
        "ret_gn_w": 1.0 + 0.05 * nrm(ks[7], (DEPTH, RET_WIDTH), f32),
        "sgu_norm_w": 1.0 + 0.05 * nrm(ks[8], (DEPTH, SGU_WIDTH), f32),
        "sgu_w": nrm(ks[9], (DEPTH, SGU_GROUPS, SGU_CHUNK, SGU_CHUNK), f32) * SGU_CHUNK ** -0.5,
        "sgu_b": 1.0 + 0.1 * nrm(ks[10], (DEPTH, SGU_GROUPS, SGU_CHUNK), f32),
        "w_out": nrm(ks[11], (DEPTH, MIX_WIDTH, D_MODEL), f32) * MIX_WIDTH ** -0.5,
        "xa_wq": nrm(ks[12], (DEPTH, D_MODEL, D_MODEL), f32) * D_MODEL ** -0.5,
        "xa_wkv": nrm(ks[13], (DEPTH, D_MODEL, 2 * D_MODEL), f32) * D_MODEL ** -0.5,
        "xa_wo": nrm(ks[14], (DEPTH, D_MODEL, D_MODEL), f32) * D_MODEL ** -0.5,
        "ffn_w_gu": nrm(ks[15], (DEPTH, D_MODEL, 2 * D_FF), f32) * D_MODEL ** -0.5,
        "ffn_w_down": nrm(ks[16], (DEPTH, D_FF, D_MODEL), f32) * D_FF ** -0.5,
    }


def reference(x_prompt, x_sample, mem_prompt, mem_sample, norm_w, w_in, ret_log_gamma, ret_gn_w,
              sgu_norm_w, sgu_w, sgu_b, w_out, xa_wq, xa_wkv, xa_wo, ffn_w_gu, ffn_w_down):
    y_prompt = x_prompt
    y_sample = x_sample
    for l in range(DEPTH):
        y_prompt = encoder_layer(y_prompt, mem_prompt, norm_w[l], w_in[l], ret_log_gamma[l], ret_gn_w[l],
                                 sgu_norm_w[l], sgu_w[l], sgu_b[l], w_out[l], xa_wq[l], xa_wkv[l], xa_wo[l],
                                 ffn_w_gu[l], ffn_w_down[l])
        y_sample = encoder_layer(y_sample, mem_sample, norm_w[l], w_in[l], ret_log_gamma[l], ret_gn_w[l],
                                 sgu_norm_w[l], sgu_w[l], sgu_b[l], w_out[l], xa_wq[l], xa_wkv[l], xa_wo[l],
                                 ffn_w_gu[l], ffn_w_down[l])
    return (y_prompt, y_sample)
```

```python
import functools
import math

import jax
import jax.numpy as jnp
from jax import lax
from jax.experimental import pallas as pl
from jax.experimental.pallas import tpu as pltpu

F32 = jnp.float32
BF16 = jnp.bfloat16

EPS = 1e-6
ROPE_BASE = 10000.0
RET_HEADS = 4
HEAD_DIM = 128
SGU_GROUPS = 4
SGU_CHUNK = 128
XA_HEADS = 4
RET_CHUNK = 256
VMEM_LIMIT_BYTES = 56 * 1024 * 1024


def _dot(a, b):
    return jnp.dot(a, b, preferred_element_type=F32)


def _rms(x, w):
    return x * lax.rsqrt(jnp.mean(x * x, axis=-1, keepdims=True) + EPS) * w


def _silu(x):
    return x * (1.0 / (1.0 + jnp.exp(-x)))


def _gelu_tanh(x):
    c = math.sqrt(2.0 / math.pi)
    return x * (0.5 * (1.0 + jnp.tanh(c * (x + 0.044715 * (x * x * x)))))


def _rope_kernel(inv_ref, cos_ref, sin_ref):
    rows = cos_ref.shape[0]
    half = HEAD_DIM // 2
    pos = (lax.broadcasted_iota(jnp.int32, (rows, HEAD_DIM), 0) + pl.program_id(0) * rows).astype(F32)
    lane = lax.broadcasted_iota(jnp.int32, (rows, HEAD_DIM), 1)
    ang = pos * inv_ref[...]
    cos_ref[...] = jnp.cos(ang)
    sin_ref[...] = jnp.where(lane < half, -1.0, 1.0) * jnp.sin(ang)


def _rope_tables(seq):
    half = HEAD_DIM // 2
    inv = ROPE_BASE ** (-jnp.arange(half, dtype=F32) / half)
    inv2 = jnp.concatenate([inv, inv]).reshape(1, HEAD_DIM)
    rows = 512
    assert seq % rows == 0
    return pl.pallas_call(
        _rope_kernel,
        grid=(seq // rows,),
        in_specs=[pl.BlockSpec((1, HEAD_DIM), lambda i: (0, 0))],
        out_specs=[pl.BlockSpec((rows, HEAD_DIM), lambda i: (i, 0))] * 2,
        out_shape=[jax.ShapeDtypeStruct((seq, HEAD_DIM), F32)] * 2,
        name="rope_tables",
    )(inv2)


def _decay_kernel(lg_ref, d_ref, tab_ref, dec_ref):
    c = d_ref.shape[1]
    ii = lax.broadcasted_iota(jnp.int32, (c, c), 0)
    jj = lax.broadcasted_iota(jnp.int32, (c, c), 1)
    dist = (ii - jj).astype(F32)
    adist = jnp.abs(dist)
    idx = lax.broadcasted_iota(jnp.int32, (c, HEAD_DIM), 0).astype(F32)
    ones = jnp.ones((1, HEAD_DIM), F32)
    for h in range(RET_HEADS):
        lf = lg_ref[0, h]
        lb = lg_ref[1, h]
        d_ref[h] = jnp.where(dist >= 0, jnp.exp(lf * adist), jnp.exp(lb * adist))
        sl = slice(h * HEAD_DIM, (h + 1) * HEAD_DIM)
        tab_ref[0, :, sl] = jnp.exp(lf * (idx + 1.0))
        tab_ref[1, :, sl] = jnp.exp(lb * (c - idx))
        tab_ref[2, :, sl] = jnp.exp(lf * (c - 1 - idx))
        tab_ref[3, :, sl] = jnp.exp(lb * idx)
        dec_ref[h:h + 1, :] = jnp.exp((lf * c) * ones)
        dec_ref[RET_HEADS + h:RET_HEADS + h + 1, :] = jnp.exp((lb * c) * ones)


def _decay_tables(log_gamma):
    c = RET_CHUNK
    width = RET_HEADS * HEAD_DIM
    return pl.pallas_call(
        _decay_kernel,
        in_specs=[pl.BlockSpec(memory_space=pltpu.SMEM)],
        out_shape=[
            jax.ShapeDtypeStruct((RET_HEADS, c, c), F32),
            jax.ShapeDtypeStruct((4, c, width), F32),
            jax.ShapeDtypeStruct((2 * RET_HEADS, HEAD_DIM), F32),
        ],
        name="decay_tables",
    )(log_gamma.astype(F32))


def _mem_kernel(mem_ref, nw_ref, wkv_ref, kt_ref, v_ref):
    d = mem_ref.shape[2]
    mn = _rms(mem_ref[0], nw_ref[...]).astype(BF16)
    kv = _dot(mn, wkv_ref[...])
    kt_ref[0] = kv[:, :d].T.astype(BF16)
    v_ref[0] = kv[:, d:].astype(BF16)


def _mem_kv(mem, nw_mem, wkv):
    b, m, d = mem.shape
    return pl.pallas_call(
        _mem_kernel,
        grid=(b,),
        in_specs=[
            pl.BlockSpec((1, m, d), lambda i: (i, 0, 0)),
            pl.BlockSpec((1, d), lambda i: (0, 0)),
            pl.BlockSpec((d, 2 * d), lambda i: (0, 0), pipeline_mode=pl.Buffered(1)),
        ],
        out_specs=[
            pl.BlockSpec((1, d, m), lambda i: (i, 0, 0)),
            pl.BlockSpec((1, m, d), lambda i: (i, 0, 0)),
        ],
        out_shape=[
            jax.ShapeDtypeStruct((b, d, m), BF16),
            jax.ShapeDtypeStruct((b, m, d), BF16),
        ],
        compiler_params=pltpu.CompilerParams(
            dimension_semantics=("arbitrary",), vmem_limit_bytes=VMEM_LIMIT_BYTES),
        name="mem_kv",
    )(mem, nw_mem, wkv)


def _mixer_kernel(x_ref, nw_ref, win_ref, cos_ref, sin_ref, d_ref, tab_ref, dec_ref,
                  sgu_nw_ref, sgu_w_ref, sgu_bt_ref,
                  qb_ref, k_ref, vb_ref, y_ref, sg_ref, sgu_ref, rf_ref):
    tm = x_ref.shape[1]
    rw = RET_HEADS * HEAD_DIM

    @pl.when(pl.program_id(1) == 0)
    def _():
        rf_ref[...] = jnp.zeros_like(rf_ref)

    h = _rms(x_ref[0], nw_ref[...]).astype(BF16)
    z = _dot(h, win_ref[...])
    cos = cos_ref[...]
    sin = sin_ref[...]
    k_scale = HEAD_DIM ** -0.5

    for hd in range(RET_HEADS):
        sl = slice(hd * HEAD_DIM, (hd + 1) * HEAD_DIM)
        q = z[:, hd * HEAD_DIM:(hd + 1) * HEAD_DIM]
        k = z[:, rw + hd * HEAD_DIM:rw + (hd + 1) * HEAD_DIM]
        v = z[:, 2 * rw + hd * HEAD_DIM:2 * rw + (hd + 1) * HEAD_DIM]
        q = q * cos + pltpu.roll(q, HEAD_DIM // 2, 1) * sin
        k = (k * cos + pltpu.roll(k, HEAD_DIM // 2, 1) * sin) * k_scale
        q16 = q.astype(BF16)
        k16 = k.astype(BF16)
        s = lax.dot_general(q16, k16, (((1,), (1,)), ((), ())), preferred_element_type=F32)
        p = (s * d_ref[hd]).astype(BF16)
        y = _dot(p, v.astype(BF16))
        qf = (q * tab_ref[0, :, sl]).astype(BF16)
        y = y + _dot(qf, rf_ref[hd].astype(BF16))
        vf = (v * tab_ref[2, :, sl]).astype(BF16)
        w = lax.dot_general(k16, vf, (((0,), (0,)), ((), ())), preferred_element_type=F32)
        rf_ref[hd] = dec_ref[hd:hd + 1, :] * rf_ref[hd] + w
        y_ref[0, :, sl] = y
        qb_ref[0, :, sl] = (q * tab_ref[1, :, sl]).astype(BF16)
        k_ref[0, :, sl] = k16
        vb_ref[0, :, sl] = (v * tab_ref[3, :, sl]).astype(BF16)

    sg_ref[0] = _silu(z[:, 3 * rw:4 * rw]).astype(BF16)

    sw = SGU_GROUPS * SGU_CHUNK
    u = _gelu_tanh(z[:, 4 * rw:4 * rw + sw])
    vs = _gelu_tanh(z[:, 4 * rw + sw:4 * rw + 2 * sw])
    mu = jnp.mean(vs, axis=-1, keepdims=True)
    vc = vs - mu
    var = jnp.mean(vc * vc, axis=-1, keepdims=True)
    vs = (vc * lax.rsqrt(var + EPS) * sgu_nw_ref[...]).astype(BF16)
    for cc in range(tm // SGU_CHUNK):
        rows = slice(cc * SGU_CHUNK, (cc + 1) * SGU_CHUNK)
        for g in range(SGU_GROUPS):
            cols = slice(g * SGU_CHUNK, (g + 1) * SGU_CHUNK)
            sp = _dot(sgu_w_ref[g], vs[rows, cols]) + sgu_bt_ref[:, g:g + 1]
            sgu_ref[0, rows, cols] = (u[rows, cols] * sp).astype(BF16)


def _mixer(x, nw_pre, w_in, cos_t, sin_t, d_t, tab_t, dec_t, sgu_nw, sgu_w, sgu_bt):
    b, s, d = x.shape
    tm = RET_CHUNK
    assert s % tm == 0
    rw = RET_HEADS * HEAD_DIM
    const2 = lambda i, j: (0, 0)
    const3 = lambda i, j: (0, 0, 0)
    tok = lambda i, j: (i, j, 0)
    out_tok = pl.BlockSpec((1, tm, rw), tok)
    return pl.pallas_call(
        _mixer_kernel,
        grid=(b, s // tm),
        in_specs=[
            pl.BlockSpec((1, tm, d), tok),
            pl.BlockSpec((1, d), const2),
            pl.BlockSpec(w_in.shape, const2, pipeline_mode=pl.Buffered(1)),
            pl.BlockSpec((tm, HEAD_DIM), lambda i, j: (j, 0)),
            pl.BlockSpec((tm, HEAD_DIM), lambda i, j: (j, 0)),
            pl.BlockSpec(d_t.shape, const3, pipeline_mode=pl.Buffered(1)),
            pl.BlockSpec(tab_t.shape, const3, pipeline_mode=pl.Buffered(1)),
            pl.BlockSpec(dec_t.shape, const2),
            pl.BlockSpec(sgu_nw.shape, const2),
            pl.BlockSpec(sgu_w.shape, const3),
            pl.BlockSpec(sgu_bt.shape, const2),
        ],
        out_specs=[out_tok] * 6,
        out_shape=[
            jax.ShapeDtypeStruct((b, s, rw), BF16),
            jax.ShapeDtypeStruct((b, s, rw), BF16),
            jax.ShapeDtypeStruct((b, s, rw), BF16),
            jax.ShapeDtypeStruct((b, s, rw), F32),
            jax.ShapeDtypeStruct((b, s, rw), BF16),
            jax.ShapeDtypeStruct((b, s, rw), BF16),
        ],
        scratch_shapes=[pltpu.VMEM((RET_HEADS, HEAD_DIM, HEAD_DIM), F32)],
        compiler_params=pltpu.CompilerParams(
            dimension_semantics=("arbitrary", "arbitrary"), vmem_limit_bytes=VMEM_LIMIT_BYTES),
        name="mixer",
    )(x, nw_pre, w_in, cos_t, sin_t, d_t, tab_t, dec_t, sgu_nw, sgu_w, sgu_bt)


def _tail_kernel(x_ref, qb_ref, k_ref, vb_ref, y_ref, sg_ref, sgu_ref, dec_ref, gnw_ref,
                 nw_ref, wout_ref, wq_ref, kt_ref, vm_ref, wo_ref, wgu_ref, wdown_ref,
                 o_ref, rb_ref):
    d_ff = wdown_ref.shape[0]
    xa_d = wq_ref.shape[1] // XA_HEADS

    @pl.when(pl.program_id(1) == 0)
    def _():
        rb_ref[...] = jnp.zeros_like(rb_ref)

    ret = []
    for hd in range(RET_HEADS):
        sl = slice(hd * HEAD_DIM, (hd + 1) * HEAD_DIM)
        y = y_ref[0, :, sl] + _dot(qb_ref[0, :, sl], rb_ref[hd].astype(BF16))
        w = lax.dot_general(k_ref[0, :, sl], vb_ref[0, :, sl], (((0,), (0,)), ((), ())),
                            preferred_element_type=F32)
        rb_ref[hd] = dec_ref[RET_HEADS + hd:RET_HEADS + hd + 1, :] * rb_ref[hd] + w
        mu = jnp.mean(y, axis=-1, keepdims=True)
        yc = y - mu
        var = jnp.mean(yc * yc, axis=-1, keepdims=True)
        yn = yc * lax.rsqrt(var + EPS) * gnw_ref[:, sl]
        ret.append((sg_ref[0, :, sl].astype(F32) * yn).astype(BF16))
    mixed = jnp.concatenate(ret + [sgu_ref[0]], axis=-1)
    x = x_ref[0]
    x = x + _rms(_dot(mixed, wout_ref[...]), nw_ref[0:1, :])

    hq = _rms(x, nw_ref[1:2, :]).astype(BF16)
    q = _dot(hq, wq_ref[...]) * (xa_d ** -0.5)
    heads = []
    for hd in range(XA_HEADS):
        sl = slice(hd * xa_d, (hd + 1) * xa_d)
        sc = _dot(q[:, sl].astype(BF16), kt_ref[0, sl, :])
        e = jnp.exp(sc - jnp.max(sc, axis=-1, keepdims=True))
        p = (e * (1.0 / jnp.sum(e, axis=-1, keepdims=True))).astype(BF16)
        heads.append(_dot(p, vm_ref[0, :, sl]).astype(BF16))
    xa = _dot(jnp.concatenate(heads, axis=-1), wo_ref[...])
    x = x + _rms(xa, nw_ref[2:3, :])

    hf = _rms(x, nw_ref[3:4, :]).astype(BF16)
    gu = _dot(hf, wgu_ref[...])
    act = (_silu(gu[:, :d_ff]) * gu[:, d_ff:]).astype(BF16)
    o_ref[0] = x + _rms(_dot(act, wdown_ref[...]), nw_ref[4:5, :])


def _tail(x, qb, k, vb, y, sg, sgu, dec_t, gn_w, nw_tail, w_out, wq, kt, vm, wo, w_gu, w_down):
    b, s, d = x.shape
    tm = RET_CHUNK
    nc = s // tm
    rw = RET_HEADS * HEAD_DIM
    n_mem = vm.shape[1]
    const2 = lambda i, j: (0, 0)
    rev = lambda i, j: (i, nc - 1 - j, 0)
    per_b = lambda i, j: (i, 0, 0)
    tok_d = pl.BlockSpec((1, tm, d), rev)
    tok_r = pl.BlockSpec((1, tm, rw), rev)
    weight = lambda w: pl.BlockSpec(w.shape, const2, pipeline_mode=pl.Buffered(1))
    return pl.pallas_call(
        _tail_kernel,
        grid=(b, nc),
        in_specs=[
            tok_d, tok_r, tok_r, tok_r, tok_r, tok_r, tok_r,
            pl.BlockSpec(dec_t.shape, const2),
            pl.BlockSpec(gn_w.shape, const2),
            pl.BlockSpec(nw_tail.shape, const2),
            weight(w_out), weight(wq),
            pl.BlockSpec((1, d, n_mem), per_b),
            pl.BlockSpec((1, n_mem, d), per_b),
            weight(wo), weight(w_gu), weight(w_down),
        ],
        out_specs=tok_d,
        out_shape=jax.ShapeDtypeStruct((b, s, d), F32),
        scratch_shapes=[pltpu.VMEM((RET_HEADS, HEAD_DIM, HEAD_DIM), F32)],
        compiler_params=pltpu.CompilerParams(
            dimension_semantics=("arbitrary", "arbitrary"), vmem_limit_bytes=VMEM_LIMIT_BYTES),
        name="tail",
    )(x, qb, k, vb, y, sg, sgu, dec_t, gn_w, nw_tail, w_out, wq, kt, vm, wo, w_gu, w_down)


def _encoder_layer(x, mem, tables, nw, w_in, gn_w, sgu_nw, sgu_w, sgu_bt, w_out, wq, wkv, wo, w_gu, w_down):
    cos_t, sin_t, d_t, tab_t, dec_t = tables
    row = lambda i: nw[i:i + 1, :]
    kt, vm = _mem_kv(mem, row(4), wkv)
    qb, k, vb, y, sg, sgu = _mixer(x, row(0), w_in, cos_t, sin_t, d_t, tab_t, dec_t, sgu_nw, sgu_w, sgu_bt)
    nw_tail = jnp.concatenate([row(1), row(2), row(3), row(5), row(6)], axis=0)
    return _tail(x, qb, k, vb, y, sg, sgu, dec_t, gn_w, nw_tail, w_out, wq, kt, vm, wo, w_gu, w_down)


def kernel(x_prompt, x_sample, mem_prompt, mem_sample, norm_w, w_in, ret_log_gamma, ret_gn_w, sgu_norm_w,
           sgu_w, sgu_b, w_out, xa_wq, xa_wkv, xa_wo, ffn_w_gu, ffn_w_down):
    depth = norm_w.shape[0]
    assert w_in.shape[2] == 4 * RET_HEADS * HEAD_DIM + 2 * SGU_GROUPS * SGU_CHUNK
    seq = max(x_prompt.shape[1], x_sample.shape[1])
    cos_t, sin_t = _rope_tables(seq)
    y_prompt, y_sample = x_prompt, x_sample
    for l in range(depth):
        d_t, tab_t, dec_t = _decay_tables(ret_log_gamma[l])
        tables = (cos_t, sin_t, d_t, tab_t, dec_t)
        args = (norm_w[l], w_in[l].astype(BF16), ret_gn_w[l].reshape(1, -1), sgu_norm_w[l].reshape(1, -1),
                sgu_w[l].astype(BF16), jnp.transpose(sgu_b[l]), w_out[l].astype(BF16), xa_wq[l].astype(BF16),
                xa_wkv[l].astype(BF16), xa_wo[l].astype(BF16), ffn_w_gu[l].astype(BF16),
                ffn_w_down[l].astype(BF16))
        y_prompt = _encoder_layer(y_prompt, mem_prompt, tables, *args)
        y_sample = _encoder_layer(y_sample, mem_sample, tables, *args)
    return (y_prompt, y_sample)
```

```python
import math

import jax
import jax.numpy as jnp
from jax import lax
from jax.experimental import pallas as pl
from jax.experimental.pallas import tpu as pltpu

F32 = jnp.float32
BF16 = jnp.bfloat16

EPS = 1e-6
ROPE_BASE = 10000.0
RET_HEADS = 4
HEAD_DIM = 128
SGU_GROUPS = 4
SGU_CHUNK = 128
XA_HEADS = 4
RET_CHUNK = 256
TOKEN_TILE = 512
VMEM_LIMIT_BYTES = 60 * 1024 * 1024


def _dot(a, b):
    return jnp.dot(a, b, preferred_element_type=F32)


def _rms(x, w):
    return x * lax.rsqrt(jnp.mean(x * x, axis=-1, keepdims=True) + EPS) * w


def _silu(x):
    return x * (1.0 / (1.0 + jnp.exp(-x)))


def _gelu_tanh(x):
    c = math.sqrt(2.0 / math.pi)
    return x * (0.5 * (1.0 + jnp.tanh(c * (x + 0.044715 * (x * x * x)))))


def _rope_kernel(inv_ref, cos_ref, sin_ref):
    rows = cos_ref.shape[0]
    half = HEAD_DIM // 2
    pos = (lax.broadcasted_iota(jnp.int32, (rows, HEAD_DIM), 0) + pl.program_id(0) * rows).astype(F32)
    lane = lax.broadcasted_iota(jnp.int32, (rows, HEAD_DIM), 1)
    ang = pos * inv_ref[...]
    cos_ref[...] = jnp.cos(ang)
    sin_ref[...] = jnp.where(lane < half, -1.0, 1.0) * jnp.sin(ang)


def _rope_tables(seq):
    half = HEAD_DIM // 2
    inv = ROPE_BASE ** (-jnp.arange(half, dtype=F32) / half)
    inv2 = jnp.concatenate([inv, inv]).reshape(1, HEAD_DIM)
    rows = 512
    assert seq % rows == 0
    return pl.pallas_call(
        _rope_kernel,
        grid=(seq // rows,),
        in_specs=[pl.BlockSpec((1, HEAD_DIM), lambda i: (0, 0))],
        out_specs=[pl.BlockSpec((rows, HEAD_DIM), lambda i: (i, 0))] * 2,
        out_shape=[jax.ShapeDtypeStruct((seq, HEAD_DIM), F32)] * 2,
        name="rope_tables",
    )(inv2)


def _decay_kernel(lg_ref, d_ref, tab_ref, dec_ref):
    c = d_ref.shape[1]
    ii = lax.broadcasted_iota(jnp.int32, (c, c), 0)
    jj = lax.broadcasted_iota(jnp.int32, (c, c), 1)
    dist = (ii - jj).astype(F32)
    adist = jnp.abs(dist)
    idx = lax.broadcasted_iota(jnp.int32, (c, HEAD_DIM), 0).astype(F32)
    ones = jnp.ones((1, HEAD_DIM), F32)
    for h in range(RET_HEADS):
        lf = lg_ref[0, h]
        lb = lg_ref[1, h]
        d_ref[h] = jnp.where(dist >= 0, jnp.exp(lf * adist), jnp.exp(lb * adist))
        sl = slice(h * HEAD_DIM, (h + 1) * HEAD_DIM)
        tab_ref[0, :, sl] = jnp.exp(lf * (idx + 1.0))
        tab_ref[1, :, sl] = jnp.exp(lb * (c - idx))
        tab_ref[2, :, sl] = jnp.exp(lf * (c - 1 - idx))
        tab_ref[3, :, sl] = jnp.exp(lb * idx)
        dec_ref[h:h + 1, :] = jnp.exp((lf * c) * ones)
        dec_ref[RET_HEADS + h:RET_HEADS + h + 1, :] = jnp.exp((lb * c) * ones)


def _decay_tables(log_gamma):
    c = RET_CHUNK
    width = RET_HEADS * HEAD_DIM
    return pl.pallas_call(
        _decay_kernel,
        in_specs=[pl.BlockSpec(memory_space=pltpu.SMEM)],
        out_shape=[
            jax.ShapeDtypeStruct((RET_HEADS, c, c), F32),
            jax.ShapeDtypeStruct((4, c, width), F32),
            jax.ShapeDtypeStruct((2 * RET_HEADS, HEAD_DIM), F32),
        ],
        name="decay_tables",
    )(log_gamma.astype(F32))


def _mem_kernel(mem_ref, nw_ref, wkv_ref, kt_ref, v_ref):
    d = mem_ref.shape[2]
    mn = _rms(mem_ref[0], nw_ref[...]).astype(BF16)
    kv = _dot(mn, wkv_ref[...])
    kt_ref[0] = kv[:, :d].T.astype(BF16)
    v_ref[0] = kv[:, d:].astype(BF16)


def _mem_kv(mem, nw_mem, wkv):
    b, m, d = mem.shape
    return pl.pallas_call(
        _mem_kernel,
        grid=(b,),
        in_specs=[
            pl.BlockSpec((1, m, d), lambda i: (i, 0, 0)),
            pl.BlockSpec((1, d), lambda i: (0, 0)),
            pl.BlockSpec((d, 2 * d), lambda i: (0, 0), pipeline_mode=pl.Buffered(1)),
        ],
        out_specs=[
            pl.BlockSpec((1, d, m), lambda i: (i, 0, 0)),
            pl.BlockSpec((1, m, d), lambda i: (i, 0, 0)),
        ],
        out_shape=[
            jax.ShapeDtypeStruct((b, d, m), BF16),
            jax.ShapeDtypeStruct((b, m, d), BF16),
        ],
        compiler_params=pltpu.CompilerParams(
            dimension_semantics=("arbitrary",), vmem_limit_bytes=VMEM_LIMIT_BYTES),
        name="mem_kv",
    )(mem, nw_mem, wkv)


def _mixer_kernel(x_ref, nw_ref, win_ref, cos_ref, sin_ref, d_ref, tab_ref, dec_ref,
                  sgu_nw_ref, sgu_w_ref, sgu_bt_ref,
                  qb_ref, k_ref, vb_ref, y_ref, sg_ref, sgu_ref, rf_ref):
    rw = RET_HEADS * HEAD_DIM
    sw = SGU_GROUPS * SGU_CHUNK
    k_scale = HEAD_DIM ** -0.5

    @pl.when(pl.program_id(1) == 0)
    def _():
        rf_ref[...] = jnp.zeros_like(rf_ref)

    tm = x_ref.shape[1]
    chunks = [slice(st * RET_CHUNK, (st + 1) * RET_CHUNK) for st in range(tm // RET_CHUNK)]
    each = lambda f, *vals: [f(*v) for v in zip(*vals)]

    h = each(lambda r: _rms(x_ref[0, r, :], nw_ref[...]).astype(BF16), chunks)
    z = each(lambda v: _dot(v, win_ref[...]), h)

    for hd in range(RET_HEADS):
        sl = slice(hd * HEAD_DIM, (hd + 1) * HEAD_DIM)
        for r, zc in zip(chunks, z):
            cos = cos_ref[r, :]
            sin = sin_ref[r, :]
            q = zc[:, hd * HEAD_DIM:(hd + 1) * HEAD_DIM]
            k = zc[:, rw + hd * HEAD_DIM:rw + (hd + 1) * HEAD_DIM]
            v = zc[:, 2 * rw + hd * HEAD_DIM:2 * rw + (hd + 1) * HEAD_DIM]
            q = q * cos + pltpu.roll(q, HEAD_DIM // 2, 1) * sin
            k = (k * cos + pltpu.roll(k, HEAD_DIM // 2, 1) * sin) * k_scale
            q16 = q.astype(BF16)
            k16 = k.astype(BF16)
            s = lax.dot_general(q16, k16, (((1,), (1,)), ((), ())), preferred_element_type=F32)
            p = (s * d_ref[hd]).astype(BF16)
            y = _dot(p, v.astype(BF16))
            qf = (q * tab_ref[0, :, sl]).astype(BF16)
            y = y + _dot(qf, rf_ref[hd].astype(BF16))
            vf = (v * tab_ref[2, :, sl]).astype(BF16)
            w = lax.dot_general(k16, vf, (((0,), (0,)), ((), ())), preferred_element_type=F32)
            rf_ref[hd] = dec_ref[hd:hd + 1, :] * rf_ref[hd] + w
            y_ref[0, r, sl] = y
            qb_ref[0, r, sl] = (q * tab_ref[1, :, sl]).astype(BF16)
            k_ref[0, r, sl] = k16
            vb_ref[0, r, sl] = (v * tab_ref[3, :, sl]).astype(BF16)

    for r, zc in zip(chunks, z):
        sg_ref[0, r, :] = _silu(zc[:, 3 * rw:4 * rw]).astype(BF16)

    def sgu_norm(zc):
        vs = _gelu_tanh(zc[:, 4 * rw + sw:4 * rw + 2 * sw])
        mu = jnp.mean(vs, axis=-1, keepdims=True)
        vc = vs - mu
        var = jnp.mean(vc * vc, axis=-1, keepdims=True)
        return (vc * lax.rsqrt(var + EPS) * sgu_nw_ref[...]).astype(BF16)

    vs = jnp.concatenate(each(sgu_norm, z), axis=0)
    n_sub = tm // SGU_CHUNK
    for g in range(SGU_GROUPS):
        cols = slice(g * SGU_CHUNK, (g + 1) * SGU_CHUNK)
        rhs = jnp.concatenate([vs[c * SGU_CHUNK:(c + 1) * SGU_CHUNK, cols] for c in range(n_sub)], axis=1)
        sp = _dot(sgu_w_ref[g], rhs) + sgu_bt_ref[:, g:g + 1]
        for c in range(n_sub):
            rows = slice(c * SGU_CHUNK, (c + 1) * SGU_CHUNK)
            zc = z[(c * SGU_CHUNK) // RET_CHUNK]
            zr = slice((c * SGU_CHUNK) % RET_CHUNK, (c * SGU_CHUNK) % RET_CHUNK + SGU_CHUNK)
            u = _gelu_tanh(zc[zr, 4 * rw + g * SGU_CHUNK:4 * rw + (g + 1) * SGU_CHUNK])
            sgu_ref[0, rows, cols] = (u * sp[:, rows]).astype(BF16)


def _mixer(x, nw_pre, w_in, cos_t, sin_t, d_t, tab_t, dec_t, sgu_nw, sgu_w, sgu_bt):
    b, s, d = x.shape
    tm = TOKEN_TILE
    assert s % tm == 0 and tm % RET_CHUNK == 0
    rw = RET_HEADS * HEAD_DIM
    const2 = lambda i, j: (0, 0)
    const3 = lambda i, j: (0, 0, 0)
    tok = lambda i, j: (i, j, 0)
    out_tok = pl.BlockSpec((1, tm, rw), tok)
    return pl.pallas_call(
        _mixer_kernel,
        grid=(b, s // tm),
        in_specs=[
            pl.BlockSpec((1, tm, d), tok),
            pl.BlockSpec((1, d), const2),
            pl.BlockSpec(w_in.shape, const2, pipeline_mode=pl.Buffered(1)),
            pl.BlockSpec((tm, HEAD_DIM), lambda i, j: (j, 0)),
            pl.BlockSpec((tm, HEAD_DIM), lambda i, j: (j, 0)),
            pl.BlockSpec(d_t.shape, const3, pipeline_mode=pl.Buffered(1)),
            pl.BlockSpec(tab_t.shape, const3, pipeline_mode=pl.Buffered(1)),
            pl.BlockSpec(dec_t.shape, const2),
            pl.BlockSpec(sgu_nw.shape, const2),
            pl.BlockSpec(sgu_w.shape, const3),
            pl.BlockSpec(sgu_bt.shape, const2),
        ],
        out_specs=[out_tok] * 6,
        out_shape=[
            jax.ShapeDtypeStruct((b, s, rw), BF16),
            jax.ShapeDtypeStruct((b, s, rw), BF16),
            jax.ShapeDtypeStruct((b, s, rw), BF16),
            jax.ShapeDtypeStruct((b, s, rw), F32),
            jax.ShapeDtypeStruct((b, s, rw), BF16),
            jax.ShapeDtypeStruct((b, s, rw), BF16),
        ],
        scratch_shapes=[pltpu.VMEM((RET_HEADS, HEAD_DIM, HEAD_DIM), F32)],
        compiler_params=pltpu.CompilerParams(
            dimension_semantics=("arbitrary", "arbitrary"), vmem_limit_bytes=VMEM_LIMIT_BYTES),
        name="mixer",
    )(x, nw_pre, w_in, cos_t, sin_t, d_t, tab_t, dec_t, sgu_nw, sgu_w, sgu_bt)


def _tail_kernel(x_ref, qb_ref, k_ref, vb_ref, y_ref, sg_ref, sgu_ref, dec_ref, gnw_ref,
                 nw_ref, wout_ref, wq_ref, kt_ref, vm_ref, wo_ref, wgu_ref, wdown_ref,
                 o_ref, rb_ref):
    d_ff = wdown_ref.shape[0]
    xa_d = wq_ref.shape[1] // XA_HEADS

    @pl.when(pl.program_id(1) == 0)
    def _():
        rb_ref[...] = jnp.zeros_like(rb_ref)

    n_chunks = x_ref.shape[1] // RET_CHUNK
    chunks = [slice(st * RET_CHUNK, (st + 1) * RET_CHUNK) for st in reversed(range(n_chunks))]
    each = lambda f, *vals: [f(*v) for v in zip(*vals)]

    def finish_retention(r):
        ret = []
        for hd in range(RET_HEADS):
            sl = slice(hd * HEAD_DIM, (hd + 1) * HEAD_DIM)
            y = y_ref[0, r, sl] + _dot(qb_ref[0, r, sl], rb_ref[hd].astype(BF16))
            w = lax.dot_general(k_ref[0, r, sl], vb_ref[0, r, sl], (((0,), (0,)), ((), ())),
                                preferred_element_type=F32)
            rb_ref[hd] = dec_ref[RET_HEADS + hd:RET_HEADS + hd + 1, :] * rb_ref[hd] + w
            mu = jnp.mean(y, axis=-1, keepdims=True)
            yc = y - mu
            var = jnp.mean(yc * yc, axis=-1, keepdims=True)
            yn = yc * lax.rsqrt(var + EPS) * gnw_ref[:, sl]
            ret.append((sg_ref[0, r, sl].astype(F32) * yn).astype(BF16))
        return jnp.concatenate(ret + [sgu_ref[0, r, :]], axis=-1)

    mixed = each(finish_retention, chunks)
    mix = each(lambda m: _dot(m, wout_ref[...]), mixed)
    x = each(lambda r, m: x_ref[0, r, :] + _rms(m, nw_ref[0:1, :]), chunks, mix)

    hq = each(lambda v: _rms(v, nw_ref[1:2, :]).astype(BF16), x)
    q = each(lambda v: _dot(v, wq_ref[...]) * (xa_d ** -0.5), hq)
    heads = [[] for _ in chunks]
    for hd in range(XA_HEADS):
        sl = slice(hd * xa_d, (hd + 1) * xa_d)
        sc = each(lambda v: _dot(v[:, sl].astype(BF16), kt_ref[0, sl, :]), q)
        e = each(lambda v: jnp.exp(v - jnp.max(v, axis=-1, keepdims=True)), sc)
        p = each(lambda v: (v * (1.0 / jnp.sum(v, axis=-1, keepdims=True))).astype(BF16), e)
        o = each(lambda v: _dot(v, vm_ref[0, :, sl]).astype(BF16), p)
        for lst, v in zip(heads, o):
            lst.append(v)
    xa = each(lambda hs: _dot(jnp.concatenate(hs, axis=-1), wo_ref[...]), heads)
    x = each(lambda v, a: v + _rms(a, nw_ref[2:3, :]), x, xa)

    hf = each(lambda v: _rms(v, nw_ref[3:4, :]).astype(BF16), x)
    gu = each(lambda v: _dot(v, wgu_ref[...]), hf)
    act = each(lambda v: (_silu(v[:, :d_ff]) * v[:, d_ff:]).astype(BF16), gu)
    ff = each(lambda v: _dot(v, wdown_ref[...]), act)
    for r, v, f in zip(chunks, x, ff):
        o_ref[0, r, :] = v + _rms(f, nw_ref[4:5, :])


def _tail(x, qb, k, vb, y, sg, sgu, dec_t, gn_w, nw_tail, w_out, wq, kt, vm, wo, w_gu, w_down):
    b, s, d = x.shape
    tm = TOKEN_TILE
    nt = s // tm
    rw = RET_HEADS * HEAD_DIM
    n_mem = vm.shape[1]
    const2 = lambda i, j: (0, 0)
    rev = lambda i, j: (i, nt - 1 - j, 0)
    per_b = lambda i, j: (i, 0, 0)
    tok_d = pl.BlockSpec((1, tm, d), rev)
    tok_r = pl.BlockSpec((1, tm, rw), rev)
    weight = lambda w: pl.BlockSpec(w.shape, const2, pipeline_mode=pl.Buffered(1))
    return pl.pallas_call(
        _tail_kernel,
        grid=(b, nt),
        in_specs=[
            tok_d, tok_r, tok_r, tok_r, tok_r, tok_r, tok_r,
            pl.BlockSpec(dec_t.shape, const2),
            pl.BlockSpec(gn_w.shape, const2),
            pl.BlockSpec(nw_tail.shape, const2),
            weight(w_out), weight(wq),
            pl.BlockSpec((1, d, n_mem), per_b),
            pl.BlockSpec((1, n_mem, d), per_b),
            weight(wo), weight(w_gu), weight(w_down),
        ],
        out_specs=tok_d,
        out_shape=jax.ShapeDtypeStruct((b, s, d), F32),
        scratch_shapes=[pltpu.VMEM((RET_HEADS, HEAD_DIM, HEAD_DIM), F32)],
        compiler_params=pltpu.CompilerParams(
            dimension_semantics=("arbitrary", "arbitrary"), vmem_limit_bytes=VMEM_LIMIT_BYTES),
        name="tail",
    )(x, qb, k, vb, y, sg, sgu, dec_t, gn_w, nw_tail, w_out, wq, kt, vm, wo, w_gu, w_down)


def _encoder_layer(x, mem, tables, nw, w_in, gn_w, sgu_nw, sgu_w, sgu_bt, w_out, wq, wkv, wo, w_gu, w_down):
    cos_t, sin_t, d_t, tab_t, dec_t = tables
    row = lambda i: nw[i:i + 1, :]
    kt, vm = _mem_kv(mem, row(4), wkv)
    qb, k, vb, y, sg, sgu = _mixer(x, row(0), w_in, cos_t, sin_t, d_t, tab_t, dec_t, sgu_nw, sgu_w, sgu_bt)
    nw_tail = jnp.concatenate([row(1), row(2), row(3), row(5), row(6)], axis=0)
    return _tail(x, qb, k, vb, y, sg, sgu, dec_t, gn_w, nw_tail, w_out, wq, kt, vm, wo, w_gu, w_down)


def kernel(x_prompt, x_sample, mem_prompt, mem_sample, norm_w, w_in, ret_log_gamma, ret_gn_w, sgu_norm_w,
           sgu_w, sgu_b, w_out, xa_wq, xa_wkv, xa_wo, ffn_w_gu, ffn_w_down):
    depth = norm_w.shape[0]
    assert w_in.shape[2] == 4 * RET_HEADS * HEAD_DIM + 2 * SGU_GROUPS * SGU_CHUNK
    seq = max(x_prompt.shape[1], x_sample.shape[1])
    cos_t, sin_t = _rope_tables(seq)
    y_prompt, y_sample = x_prompt, x_sample
    for l in range(depth):
        d_t, tab_t, dec_t = _decay_tables(ret_log_gamma[l])
        tables = (cos_t, sin_t, d_t, tab_t, dec_t)
        args = (norm_w[l], w_in[l].astype(BF16), ret_gn_w[l].reshape(1, -1), sgu_norm_w[l].reshape(1, -1),
                sgu_w[l].astype(BF16), jnp.transpose(sgu_b[l]), w_out[l].astype(BF16), xa_wq[l].astype(BF16),
                xa_wkv[l].astype(BF16), xa_wo[l].astype(BF16), ffn_w_gu[l].astype(BF16),
                ffn_w_down[l].astype(BF16))
        y_prompt = _encoder_layer(y_prompt, mem_prompt, tables, *args)
        y_sample = _encoder_layer(y_sample, mem_sample, tables, *args)
    return (y_prompt, y_sample)
```

```python
import math

import jax
import jax.numpy as jnp
from jax import lax
from jax.experimental import pallas as pl
from jax.experimental.pallas import tpu as pltpu

F32 = jnp.float32
BF16 = jnp.bfloat16

EPS = 1e-6
ROPE_BASE = 10000.0
RET_HEADS = 4
HEAD_DIM = 128
SGU_GROUPS = 4
SGU_CHUNK = 128
XA_HEADS = 4
RET_CHUNK = 256
TOKEN_TILE = 512
VMEM_LIMIT_BYTES = 60 * 1024 * 1024
MXU_WIDTH = 256
MIXER_ORDER = (0, 0, 1, 0, 1, 0, 1, 0, 1, 0, 1, 1, 0, 1, 0, 1)
TAIL_ORDER = (0, 1, 0, 1, 0, 0, 1, 1, 0, 0, 1, 1, 0, 0, 1, 0, 0, 0, 1, 1, 1, 1, 0, 1, 0, 1)


def _dot(a, b):
    return jnp.dot(a, b, preferred_element_type=F32)


def _rms(x, w):
    return x * lax.rsqrt(jnp.mean(x * x, axis=-1, keepdims=True) + EPS) * w


def _silu(x):
    return x * (1.0 / (1.0 + jnp.exp(-x)))


def FFN_BLOCKS(d_ff):
    step = 3 * MXU_WIDTH
    return [(lo, min(lo + step, d_ff)) for lo in range(0, d_ff, step)]


def _run_ordered(stage_iters, order):
    for i in order:
        next(stage_iters[i], None)
    for it in stage_iters:
        assert next(it, "end") == "end", "stage order does not cover every stage"


def _gelu_tanh(x):
    c = math.sqrt(2.0 / math.pi)
    return x * (0.5 * (1.0 + jnp.tanh(c * (x + 0.044715 * (x * x * x)))))


def _rope_kernel(inv_ref, cos_ref, sin_ref):
    rows = cos_ref.shape[0]
    half = HEAD_DIM // 2
    pos = (lax.broadcasted_iota(jnp.int32, (rows, HEAD_DIM), 0) + pl.program_id(0) * rows).astype(F32)
    lane = lax.broadcasted_iota(jnp.int32, (rows, HEAD_DIM), 1)
    ang = pos * inv_ref[...]
    cos_ref[...] = jnp.cos(ang)
    sin_ref[...] = jnp.where(lane < half, -1.0, 1.0) * jnp.sin(ang)


def _rope_tables(seq):
    half = HEAD_DIM // 2
    inv = ROPE_BASE ** (-jnp.arange(half, dtype=F32) / half)
    inv2 = jnp.concatenate([inv, inv]).reshape(1, HEAD_DIM)
    rows = 512
    assert seq % rows == 0
    return pl.pallas_call(
        _rope_kernel,
        grid=(seq // rows,),
        in_specs=[pl.BlockSpec((1, HEAD_DIM), lambda i: (0, 0))],
        out_specs=[pl.BlockSpec((rows, HEAD_DIM), lambda i: (i, 0))] * 2,
        out_shape=[jax.ShapeDtypeStruct((seq, HEAD_DIM), F32)] * 2,
        name="rope_tables",
    )(inv2)


def _decay_kernel(lg_ref, d_ref, tab_ref, dec_ref):
    c = d_ref.shape[1]
    ii = lax.broadcasted_iota(jnp.int32, (c, c), 0)
    jj = lax.broadcasted_iota(jnp.int32, (c, c), 1)
    dist = (ii - jj).astype(F32)
    adist = jnp.abs(dist)
    idx = lax.broadcasted_iota(jnp.int32, (c, HEAD_DIM), 0).astype(F32)
    ones = jnp.ones((1, HEAD_DIM), F32)
    for h in range(RET_HEADS):
        lf = lg_ref[0, h]
        lb = lg_ref[1, h]
        d_ref[h] = jnp.where(dist >= 0, jnp.exp(lf * adist), jnp.exp(lb * adist))
        sl = slice(h * HEAD_DIM, (h + 1) * HEAD_DIM)
        tab_ref[0, :, sl] = jnp.exp(lf * (idx + 1.0))
        tab_ref[1, :, sl] = jnp.exp(lb * (c - idx))
        tab_ref[2, :, sl] = jnp.exp(lf * (c - 1 - idx))
        tab_ref[3, :, sl] = jnp.exp(lb * idx)
        dec_ref[h:h + 1, :] = jnp.exp((lf * c) * ones)
        dec_ref[RET_HEADS + h:RET_HEADS + h + 1, :] = jnp.exp((lb * c) * ones)


def _decay_tables(log_gamma):
    c = RET_CHUNK
    width = RET_HEADS * HEAD_DIM
    return pl.pallas_call(
        _decay_kernel,
        in_specs=[pl.BlockSpec(memory_space=pltpu.SMEM)],
        out_shape=[
            jax.ShapeDtypeStruct((RET_HEADS, c, c), F32),
            jax.ShapeDtypeStruct((4, c, width), F32),
            jax.ShapeDtypeStruct((2 * RET_HEADS, HEAD_DIM), F32),
        ],
        name="decay_tables",
    )(log_gamma.astype(F32))


def _mem_kernel(mem_ref, nw_ref, wkv_ref, kt_ref, v_ref):
    d = mem_ref.shape[2]
    mn = _rms(mem_ref[0], nw_ref[...]).astype(BF16)
    kv = _dot(mn, wkv_ref[...])
    kt_ref[0] = kv[:, :d].T.astype(BF16)
    v_ref[0] = kv[:, d:].astype(BF16)


def _mem_kv(mem, nw_mem, wkv):
    b, m, d = mem.shape
    return pl.pallas_call(
        _mem_kernel,
        grid=(b,),
        in_specs=[
            pl.BlockSpec((1, m, d), lambda i: (i, 0, 0)),
            pl.BlockSpec((1, d), lambda i: (0, 0)),
            pl.BlockSpec((d, 2 * d), lambda i: (0, 0), pipeline_mode=pl.Buffered(1)),
        ],
        out_specs=[
            pl.BlockSpec((1, d, m), lambda i: (i, 0, 0)),
            pl.BlockSpec((1, m, d), lambda i: (i, 0, 0)),
        ],
        out_shape=[
            jax.ShapeDtypeStruct((b, d, m), BF16),
            jax.ShapeDtypeStruct((b, m, d), BF16),
        ],
        compiler_params=pltpu.CompilerParams(
            dimension_semantics=("arbitrary",), vmem_limit_bytes=VMEM_LIMIT_BYTES),
        name="mem_kv",
    )(mem, nw_mem, wkv)


def _mixer_kernel(x_ref, nw_ref, win_ref, cos_ref, sin_ref, d_ref, tab_ref, dec_ref,
                  sgu_nw_ref, sgu_w_ref, sgu_bt_ref,
                  qb_ref, k_ref, vb_ref, y_ref, sg_ref, sgu_ref, rf_ref):
    rw = RET_HEADS * HEAD_DIM
    sw = SGU_GROUPS * SGU_CHUNK
    k_scale = HEAD_DIM ** -0.5

    @pl.when(pl.program_id(1) == 0)
    def _():
        rf_ref[...] = jnp.zeros_like(rf_ref)

    def chunk_stages(r):
        h = _rms(x_ref[0, r, :], nw_ref[...]).astype(BF16)
        yield
        zr = _dot(h, win_ref[:, :3 * rw])
        yield
        zg = _dot(h, win_ref[:, 3 * rw:])
        yield
        cos = cos_ref[r, :]
        sin = sin_ref[r, :]
        qs, ks, vs = [], [], []
        for hd in range(RET_HEADS):
            sl = slice(hd * HEAD_DIM, (hd + 1) * HEAD_DIM)
            q = zr[:, hd * HEAD_DIM:(hd + 1) * HEAD_DIM]
            k = zr[:, rw + hd * HEAD_DIM:rw + (hd + 1) * HEAD_DIM]
            v = zr[:, 2 * rw + hd * HEAD_DIM:2 * rw + (hd + 1) * HEAD_DIM]
            q = q * cos + pltpu.roll(q, HEAD_DIM // 2, 1) * sin
            k = (k * cos + pltpu.roll(k, HEAD_DIM // 2, 1) * sin) * k_scale
            k16 = k.astype(BF16)
            qb_ref[0, r, sl] = (q * tab_ref[1, :, sl]).astype(BF16)
            k_ref[0, r, sl] = k16
            vb_ref[0, r, sl] = (v * tab_ref[3, :, sl]).astype(BF16)
            qs.append(q)
            ks.append(k16)
            vs.append(v)
        yield
        sg_ref[0, r, :] = _silu(zg[:, :rw]).astype(BF16)
        u = _gelu_tanh(zg[:, rw:rw + sw])
        vg = _gelu_tanh(zg[:, rw + sw:rw + 2 * sw])
        mu = jnp.mean(vg, axis=-1, keepdims=True)
        vc = vg - mu
        var = jnp.mean(vc * vc, axis=-1, keepdims=True)
        vn = (vc * lax.rsqrt(var + EPS) * sgu_nw_ref[...]).astype(BF16)
        yield
        scores, y_state = [], []
        for hd in range(RET_HEADS):
            sl = slice(hd * HEAD_DIM, (hd + 1) * HEAD_DIM)
            q, k16, v = qs[hd], ks[hd], vs[hd]
            qf = (q * tab_ref[0, :, sl]).astype(BF16)
            y_state.append(_dot(qf, rf_ref[hd].astype(BF16)))
            vf = (v * tab_ref[2, :, sl]).astype(BF16)
            w = lax.dot_general(k16, vf, (((0,), (0,)), ((), ())), preferred_element_type=F32)
            rf_ref[hd] = dec_ref[hd:hd + 1, :] * rf_ref[hd] + w
            scores.append(lax.dot_general(q.astype(BF16), k16, (((1,), (1,)), ((), ())),
                                          preferred_element_type=F32))
        yield
        for hd in range(RET_HEADS):
            sl = slice(hd * HEAD_DIM, (hd + 1) * HEAD_DIM)
            p = (scores[hd] * d_ref[hd]).astype(BF16)
            y_ref[0, r, sl] = _dot(p, vs[hd].astype(BF16)) + y_state[hd]
        yield
        n_sub = RET_CHUNK // SGU_CHUNK
        for g in range(SGU_GROUPS):
            cols = slice(g * SGU_CHUNK, (g + 1) * SGU_CHUNK)
            rhs = jnp.concatenate([vn[c * SGU_CHUNK:(c + 1) * SGU_CHUNK, cols] for c in range(n_sub)], axis=1)
            sp = _dot(sgu_w_ref[g], rhs) + sgu_bt_ref[:, g:g + 1]
            for c in range(n_sub):
                rows = slice(c * SGU_CHUNK, (c + 1) * SGU_CHUNK)
                sgu_ref[0, r.start + c * SGU_CHUNK:r.start + (c + 1) * SGU_CHUNK, cols] = (
                    u[rows, cols] * sp[:, rows]).astype(BF16)

    n_chunks = x_ref.shape[1] // RET_CHUNK
    _run_ordered([chunk_stages(slice(st * RET_CHUNK, (st + 1) * RET_CHUNK)) for st in range(n_chunks)],
                 MIXER_ORDER)


def _mixer(x, nw_pre, w_in, cos_t, sin_t, d_t, tab_t, dec_t, sgu_nw, sgu_w, sgu_bt):
    b, s, d = x.shape
    tm = TOKEN_TILE
    assert s % tm == 0 and tm % RET_CHUNK == 0
    rw = RET_HEADS * HEAD_DIM
    const2 = lambda i, j: (0, 0)
    const3 = lambda i, j: (0, 0, 0)
    tok = lambda i, j: (i, j, 0)
    out_tok = pl.BlockSpec((1, tm, rw), tok)
    return pl.pallas_call(
        _mixer_kernel,
        grid=(b, s // tm),
        in_specs=[
            pl.BlockSpec((1, tm, d), tok),
            pl.BlockSpec((1, d), const2),
            pl.BlockSpec(w_in.shape, const2, pipeline_mode=pl.Buffered(1)),
            pl.BlockSpec((tm, HEAD_DIM), lambda i, j: (j, 0)),
            pl.BlockSpec((tm, HEAD_DIM), lambda i, j: (j, 0)),
            pl.BlockSpec(d_t.shape, const3, pipeline_mode=pl.Buffered(1)),
            pl.BlockSpec(tab_t.shape, const3, pipeline_mode=pl.Buffered(1)),
            pl.BlockSpec(dec_t.shape, const2),
            pl.BlockSpec(sgu_nw.shape, const2),
            pl.BlockSpec(sgu_w.shape, const3),
            pl.BlockSpec(sgu_bt.shape, const2),
        ],
        out_specs=[out_tok] * 6,
        out_shape=[
            jax.ShapeDtypeStruct((b, s, rw), BF16),
            jax.ShapeDtypeStruct((b, s, rw), BF16),
            jax.ShapeDtypeStruct((b, s, rw), BF16),
            jax.ShapeDtypeStruct((b, s, rw), F32),
            jax.ShapeDtypeStruct((b, s, rw), BF16),
            jax.ShapeDtypeStruct((b, s, rw), BF16),
        ],
        scratch_shapes=[pltpu.VMEM((RET_HEADS, HEAD_DIM, HEAD_DIM), F32)],
        compiler_params=pltpu.CompilerParams(
            dimension_semantics=("arbitrary", "arbitrary"), vmem_limit_bytes=VMEM_LIMIT_BYTES),
        name="mixer",
    )(x, nw_pre, w_in, cos_t, sin_t, d_t, tab_t, dec_t, sgu_nw, sgu_w, sgu_bt)


def _tail_kernel(x_ref, qb_ref, k_ref, vb_ref, y_ref, sg_ref, sgu_ref, dec_ref, gnw_ref,
                 nw_ref, wout_ref, wq_ref, kt_ref, vm_ref, wo_ref, wgu_ref, wdown_ref,
                 o_ref, rb_ref):
    d_ff = wdown_ref.shape[0]
    xa_d = wq_ref.shape[1] // XA_HEADS

    @pl.when(pl.program_id(1) == 0)
    def _():
        rb_ref[...] = jnp.zeros_like(rb_ref)

    def chunk_stages(r):
        ret = []
        for hd in range(RET_HEADS):
            sl = slice(hd * HEAD_DIM, (hd + 1) * HEAD_DIM)
            y = y_ref[0, r, sl] + _dot(qb_ref[0, r, sl], rb_ref[hd].astype(BF16))
            w = lax.dot_general(k_ref[0, r, sl], vb_ref[0, r, sl], (((0,), (0,)), ((), ())),
                                preferred_element_type=F32)
            rb_ref[hd] = dec_ref[RET_HEADS + hd:RET_HEADS + hd + 1, :] * rb_ref[hd] + w
            mu = jnp.mean(y, axis=-1, keepdims=True)
            yc = y - mu
            var = jnp.mean(yc * yc, axis=-1, keepdims=True)
            yn = yc * lax.rsqrt(var + EPS) * gnw_ref[:, sl]
            ret.append((sg_ref[0, r, sl].astype(F32) * yn).astype(BF16))
        mixed = jnp.concatenate(ret + [sgu_ref[0, r, :]], axis=-1)
        yield
        mix = _dot(mixed, wout_ref[...])
        yield
        x = x_ref[0, r, :] + _rms(mix, nw_ref[0:1, :])
        hq = _rms(x, nw_ref[1:2, :]).astype(BF16)
        yield
        q = _dot(hq, wq_ref[...]) * (xa_d ** -0.5)
        head_cols = [slice(hd * xa_d, (hd + 1) * xa_d) for hd in range(XA_HEADS)]
        scores = [_dot(q[:, sl].astype(BF16), kt_ref[0, sl, :]) for sl in head_cols]
        yield
        heads = []
        for sl, sc in zip(head_cols, scores):
            e = jnp.exp(sc - jnp.max(sc, axis=-1, keepdims=True))
            p = (e * (1.0 / jnp.sum(e, axis=-1, keepdims=True))).astype(BF16)
            heads.append(_dot(p, vm_ref[0, :, sl]).astype(BF16))
        yield
        xa = _dot(jnp.concatenate(heads, axis=-1), wo_ref[...])
        yield
        x = x + _rms(xa, nw_ref[2:3, :])
        hf = _rms(x, nw_ref[3:4, :]).astype(BF16)
        yield
        acts = []
        for lo, hi in FFN_BLOCKS(d_ff):
            g = _dot(hf, wgu_ref[:, lo:hi])
            u = _dot(hf, wgu_ref[:, d_ff + lo:d_ff + hi])
            acts.append((_silu(g) * u).astype(BF16))
            yield
        ff = _dot(jnp.concatenate(acts, axis=-1), wdown_ref[...])
        yield
        o_ref[0, r, :] = x + _rms(ff, nw_ref[4:5, :])

    n_chunks = x_ref.shape[1] // RET_CHUNK
    _run_ordered([chunk_stages(slice(st * RET_CHUNK, (st + 1) * RET_CHUNK))
                  for st in reversed(range(n_chunks))], TAIL_ORDER)


def _tail(x, qb, k, vb, y, sg, sgu, dec_t, gn_w, nw_tail, w_out, wq, kt, vm, wo, w_gu, w_down):
    b, s, d = x.shape
    tm = TOKEN_TILE
    nt = s // tm
    rw = RET_HEADS * HEAD_DIM
    n_mem = vm.shape[1]
    const2 = lambda i, j: (0, 0)
    rev = lambda i, j: (i, nt - 1 - j, 0)
    per_b = lambda i, j: (i, 0, 0)
    tok_d = pl.BlockSpec((1, tm, d), rev)
    tok_r = pl.BlockSpec((1, tm, rw), rev)
    weight = lambda w: pl.BlockSpec(w.shape, const2, pipeline_mode=pl.Buffered(1))
    return pl.pallas_call(
        _tail_kernel,
        grid=(b, nt),
        in_specs=[
            tok_d, tok_r, tok_r, tok_r, tok_r, tok_r, tok_r,
            pl.BlockSpec(dec_t.shape, const2),
            pl.BlockSpec(gn_w.shape, const2),
            pl.BlockSpec(nw_tail.shape, const2),
            weight(w_out), weight(wq),
            pl.BlockSpec((1, d, n_mem), per_b),
            pl.BlockSpec((1, n_mem, d), per_b),
            weight(wo), weight(w_gu), weight(w_down),
        ],
        out_specs=tok_d,
        out_shape=jax.ShapeDtypeStruct((b, s, d), F32),
        scratch_shapes=[pltpu.VMEM((RET_HEADS, HEAD_DIM, HEAD_DIM), F32)],
        compiler_params=pltpu.CompilerParams(
            dimension_semantics=("arbitrary", "arbitrary"), vmem_limit_bytes=VMEM_LIMIT_BYTES),
        name="tail",
    )(x, qb, k, vb, y, sg, sgu, dec_t, gn_w, nw_tail, w_out, wq, kt, vm, wo, w_gu, w_down)


def _encoder_layer(x, mem, tables, nw, w_in, gn_w, sgu_nw, sgu_w, sgu_bt, w_out, wq, wkv, wo, w_gu, w_down):
    cos_t, sin_t, d_t, tab_t, dec_t = tables
    row = lambda i: nw[i:i + 1, :]
    kt, vm = _mem_kv(mem, row(4), wkv)
    qb, k, vb, y, sg, sgu = _mixer(x, row(0), w_in, cos_t, sin_t, d_t, tab_t, dec_t, sgu_nw, sgu_w, sgu_bt)
    nw_tail = jnp.concatenate([row(1), row(2), row(3), row(5), row(6)], axis=0)
    return _tail(x, qb, k, vb, y, sg, sgu, dec_t, gn_w, nw_tail, w_out, wq, kt, vm, wo, w_gu, w_down)


def kernel(x_prompt, x_sample, mem_prompt, mem_sample, norm_w, w_in, ret_log_gamma, ret_gn_w, sgu_norm_w,
           sgu_w, sgu_b, w_out, xa_wq, xa_wkv, xa_wo, ffn_w_gu, ffn_w_down):
    depth = norm_w.shape[0]
    assert w_in.shape[2] == 4 * RET_HEADS * HEAD_DIM + 2 * SGU_GROUPS * SGU_CHUNK
    seq = max(x_prompt.shape[1], x_sample.shape[1])
    cos_t, sin_t = _rope_tables(seq)
    y_prompt, y_sample = x_prompt, x_sample
    for l in range(depth):
        d_t, tab_t, dec_t = _decay_tables(ret_log_gamma[l])
        tables = (cos_t, sin_t, d_t, tab_t, dec_t)
        args = (norm_w[l], w_in[l].astype(BF16), ret_gn_w[l].reshape(1, -1), sgu_norm_w[l].reshape(1, -1),
                sgu_w[l].astype(BF16), jnp.transpose(sgu_b[l]), w_out[l].astype(BF16), xa_wq[l].astype(BF16),
                xa_wkv[l].astype(BF16), xa_wo[l].astype(BF16), ffn_w_gu[l].astype(BF16),
                ffn_w_down[l].astype(BF16))
        y_prompt = _encoder_layer(y_prompt, mem_prompt, tables, *args)
        y_sample = _encoder_layer(y_sample, mem_sample, tables, *args)
    return (y_prompt, y_sample)
```

```python
import math

import jax
import jax.numpy as jnp
from jax import lax
from jax.experimental import pallas as pl
from jax.experimental.pallas import tpu as pltpu

F32 = jnp.float32
BF16 = jnp.bfloat16

EPS = 1e-6
ROPE_BASE = 10000.0
RET_HEADS = 4
HEAD_DIM = 128
SGU_GROUPS = 4
SGU_CHUNK = 128
XA_HEADS = 4
RET_CHUNK = 256
TOKEN_TILE = 512
MIXER_TILE = 1024
VMEM_LIMIT_BYTES = 60 * 1024 * 1024
MXU_WIDTH = 256
TAIL_ORDER = (0, 1, 0, 1, 0, 0, 1, 1, 0, 0, 1, 1, 0, 0, 1, 0, 0, 0, 1, 1, 1, 1, 0, 1, 0, 1)


def _mixer_order(n_chunks):
    order = [0, 0, 0]
    for c in range(n_chunks):
        nxt = [c + 1] if c + 1 < n_chunks else []
        order += nxt + [c, c] + nxt + [c, c] + nxt + [c]
    return order


def _dot(a, b):
    return jnp.dot(a, b, preferred_element_type=F32)


def _rms(x, w):
    return x * lax.rsqrt(jnp.mean(x * x, axis=-1, keepdims=True) + EPS) * w


def _silu(x):
    return x * (1.0 / (1.0 + jnp.exp(-x)))


def FFN_BLOCKS(d_ff):
    step = 3 * MXU_WIDTH
    return [(lo, min(lo + step, d_ff)) for lo in range(0, d_ff, step)]


def _run_ordered(stage_iters, order):
    for i in order:
        next(stage_iters[i], None)
    for it in stage_iters:
        assert next(it, "end") == "end", "stage order does not cover every stage"


def _gelu_tanh(x):
    c = math.sqrt(2.0 / math.pi)
    return x * (0.5 * (1.0 + jnp.tanh(c * (x + 0.044715 * (x * x * x)))))


def _rope_kernel(inv_ref, cos_ref, sin_ref):
    rows = cos_ref.shape[0]
    half = HEAD_DIM // 2
    pos = (lax.broadcasted_iota(jnp.int32, (rows, HEAD_DIM), 0) + pl.program_id(0) * rows).astype(F32)
    lane = lax.broadcasted_iota(jnp.int32, (rows, HEAD_DIM), 1)
    ang = pos * inv_ref[...]
    cos_ref[...] = jnp.cos(ang)
    sin_ref[...] = jnp.where(lane < half, -1.0, 1.0) * jnp.sin(ang)


def _rope_tables(seq):
    half = HEAD_DIM // 2
    inv = ROPE_BASE ** (-jnp.arange(half, dtype=F32) / half)
    inv2 = jnp.concatenate([inv, inv]).reshape(1, HEAD_DIM)
    rows = 512
    assert seq % rows == 0
    return pl.pallas_call(
        _rope_kernel,
        grid=(seq // rows,),
        in_specs=[pl.BlockSpec((1, HEAD_DIM), lambda i: (0, 0))],
        out_specs=[pl.BlockSpec((rows, HEAD_DIM), lambda i: (i, 0))] * 2,
        out_shape=[jax.ShapeDtypeStruct((seq, HEAD_DIM), F32)] * 2,
        name="rope_tables",
    )(inv2)


def _decay_kernel(lg_ref, d_ref, tab_ref, dec_ref):
    c = d_ref.shape[1]
    ii = lax.broadcasted_iota(jnp.int32, (c, c), 0)
    jj = lax.broadcasted_iota(jnp.int32, (c, c), 1)
    dist = (ii - jj).astype(F32)
    adist = jnp.abs(dist)
    idx = lax.broadcasted_iota(jnp.int32, (c, HEAD_DIM), 0).astype(F32)
    ones = jnp.ones((1, HEAD_DIM), F32)
    for h in range(RET_HEADS):
        lf = lg_ref[0, h]
        lb = lg_ref[1, h]
        d_ref[h] = jnp.where(dist >= 0, jnp.exp(lf * adist), jnp.exp(lb * adist))
        sl = slice(h * HEAD_DIM, (h + 1) * HEAD_DIM)
        tab_ref[0, :, sl] = jnp.exp(lf * (idx + 1.0))
        tab_ref[1, :, sl] = jnp.exp(lb * (c - idx))
        tab_ref[2, :, sl] = jnp.exp(lf * (c - 1 - idx))
        tab_ref[3, :, sl] = jnp.exp(lb * idx)
        dec_ref[h:h + 1, :] = jnp.exp((lf * c) * ones)
        dec_ref[RET_HEADS + h:RET_HEADS + h + 1, :] = jnp.exp((lb * c) * ones)


def _decay_tables(log_gamma):
    c = RET_CHUNK
    width = RET_HEADS * HEAD_DIM
    return pl.pallas_call(
        _decay_kernel,
        in_specs=[pl.BlockSpec(memory_space=pltpu.SMEM)],
        out_shape=[
            jax.ShapeDtypeStruct((RET_HEADS, c, c), F32),
            jax.ShapeDtypeStruct((4, c, width), F32),
            jax.ShapeDtypeStruct((2 * RET_HEADS, HEAD_DIM), F32),
        ],
        name="decay_tables",
    )(log_gamma.astype(F32))


def _mem_kernel(mem_ref, nw_ref, wkv_ref, kt_ref, v_ref):
    d = mem_ref.shape[2]
    mn = _rms(mem_ref[0], nw_ref[...]).astype(BF16)
    kv = _dot(mn, wkv_ref[...])
    kt_ref[0] = kv[:, :d].T.astype(BF16)
    v_ref[0] = kv[:, d:].astype(BF16)


def _mem_kv(mem, nw_mem, wkv):
    b, m, d = mem.shape
    return pl.pallas_call(
        _mem_kernel,
        grid=(b,),
        in_specs=[
            pl.BlockSpec((1, m, d), lambda i: (i, 0, 0)),
            pl.BlockSpec((1, d), lambda i: (0, 0)),
            pl.BlockSpec((d, 2 * d), lambda i: (0, 0), pipeline_mode=pl.Buffered(1)),
        ],
        out_specs=[
            pl.BlockSpec((1, d, m), lambda i: (i, 0, 0)),
            pl.BlockSpec((1, m, d), lambda i: (i, 0, 0)),
        ],
        out_shape=[
            jax.ShapeDtypeStruct((b, d, m), BF16),
            jax.ShapeDtypeStruct((b, m, d), BF16),
        ],
        compiler_params=pltpu.CompilerParams(
            dimension_semantics=("arbitrary",), vmem_limit_bytes=VMEM_LIMIT_BYTES),
        name="mem_kv",
    )(mem, nw_mem, wkv)


def _mixer_kernel(x_ref, nw_ref, win_ref, cos_ref, sin_ref, d_ref, tab_ref, dec_ref,
                  sgu_nw_ref, sgu_w_ref, sgu_bt_ref,
                  qb_ref, wb_ref, y_ref, sg_ref, sgu_ref, rf_ref):
    rw = RET_HEADS * HEAD_DIM
    sw = SGU_GROUPS * SGU_CHUNK
    k_scale = HEAD_DIM ** -0.5

    @pl.when(pl.program_id(1) == 0)
    def _():
        rf_ref[...] = jnp.zeros_like(rf_ref)

    def chunk_stages(r):
        h = _rms(x_ref[0, r, :], nw_ref[...]).astype(BF16)
        yield
        zr = _dot(h, win_ref[:, :3 * rw])
        yield
        zg = _dot(h, win_ref[:, 3 * rw:])
        yield
        cos = cos_ref[r, :]
        sin = sin_ref[r, :]
        qs, ks, vs = [], [], []
        for hd in range(RET_HEADS):
            sl = slice(hd * HEAD_DIM, (hd + 1) * HEAD_DIM)
            q = zr[:, hd * HEAD_DIM:(hd + 1) * HEAD_DIM]
            k = zr[:, rw + hd * HEAD_DIM:rw + (hd + 1) * HEAD_DIM]
            v = zr[:, 2 * rw + hd * HEAD_DIM:2 * rw + (hd + 1) * HEAD_DIM]
            q = q * cos + pltpu.roll(q, HEAD_DIM // 2, 1) * sin
            k = (k * cos + pltpu.roll(k, HEAD_DIM // 2, 1) * sin) * k_scale
            k16 = k.astype(BF16)
            qb_ref[0, r, sl] = (q * tab_ref[1, :, sl]).astype(BF16)
            qs.append(q)
            ks.append(k16)
            vs.append(v)
        yield
        scores, y_state = [], []
        for hd in range(RET_HEADS):
            sl = slice(hd * HEAD_DIM, (hd + 1) * HEAD_DIM)
            q, k16, v = qs[hd], ks[hd], vs[hd]
            qf = (q * tab_ref[0, :, sl]).astype(BF16)
            y_state.append(_dot(qf, rf_ref[hd].astype(BF16)))
            vfb = jnp.concatenate([v * tab_ref[2, :, sl], v * tab_ref[3, :, sl]], axis=1).astype(BF16)
            w = lax.dot_general(k16, vfb, (((0,), (0,)), ((), ())), preferred_element_type=F32)
            rf_ref[hd] = dec_ref[hd:hd + 1, :] * rf_ref[hd] + w[:, :HEAD_DIM]
            wb_ref[0, r.start // RET_CHUNK, sl, :] = w[:, HEAD_DIM:]
            scores.append(lax.dot_general(q.astype(BF16), k16, (((1,), (1,)), ((), ())),
                                          preferred_element_type=F32))
        yield
        sg_ref[0, r, :] = _silu(zg[:, :rw]).astype(BF16)
        u = _gelu_tanh(zg[:, rw:rw + sw])
        vg = _gelu_tanh(zg[:, rw + sw:rw + 2 * sw])
        mu = jnp.mean(vg, axis=-1, keepdims=True)
        vc = vg - mu
        var = jnp.mean(vc * vc, axis=-1, keepdims=True)
        vn = (vc * lax.rsqrt(var + EPS) * sgu_nw_ref[...]).astype(BF16)
        yield
        for hd in range(RET_HEADS):
            sl = slice(hd * HEAD_DIM, (hd + 1) * HEAD_DIM)
            p = (scores[hd] * d_ref[hd]).astype(BF16)
            y_ref[0, r, sl] = _dot(p, vs[hd].astype(BF16)) + y_state[hd]
        yield
        n_sub = RET_CHUNK // SGU_CHUNK
        for g in range(SGU_GROUPS):
            cols = slice(g * SGU_CHUNK, (g + 1) * SGU_CHUNK)
            rhs = jnp.concatenate([vn[c * SGU_CHUNK:(c + 1) * SGU_CHUNK, cols] for c in range(n_sub)], axis=1)
            sp = _dot(sgu_w_ref[g], rhs) + sgu_bt_ref[:, g:g + 1]
            for c in range(n_sub):
                rows = slice(c * SGU_CHUNK, (c + 1) * SGU_CHUNK)
                sgu_ref[0, r.start + c * SGU_CHUNK:r.start + (c + 1) * SGU_CHUNK, cols] = (
                    u[rows, cols] * sp[:, rows]).astype(BF16)

    n_chunks = x_ref.shape[1] // RET_CHUNK
    _run_ordered([chunk_stages(slice(st * RET_CHUNK, (st + 1) * RET_CHUNK)) for st in range(n_chunks)],
                 _mixer_order(n_chunks))


def _mixer(x, nw_pre, w_in, cos_t, sin_t, d_t, tab_t, dec_t, sgu_nw, sgu_w, sgu_bt):
    b, s, d = x.shape
    tm = MIXER_TILE
    assert s % tm == 0 and tm % RET_CHUNK == 0
    rw = RET_HEADS * HEAD_DIM
    const2 = lambda i, j: (0, 0)
    const3 = lambda i, j: (0, 0, 0)
    tok = lambda i, j: (i, j, 0)
    out_tok = pl.BlockSpec((1, tm, rw), tok)
    return pl.pallas_call(
        _mixer_kernel,
        grid=(b, s // tm),
        in_specs=[
            pl.BlockSpec((1, tm, d), tok),
            pl.BlockSpec((1, d), const2),
            pl.BlockSpec(w_in.shape, const2, pipeline_mode=pl.Buffered(1)),
            pl.BlockSpec((tm, HEAD_DIM), lambda i, j: (j, 0)),
            pl.BlockSpec((tm, HEAD_DIM), lambda i, j: (j, 0)),
            pl.BlockSpec(d_t.shape, const3, pipeline_mode=pl.Buffered(1)),
            pl.BlockSpec(tab_t.shape, const3, pipeline_mode=pl.Buffered(1)),
            pl.BlockSpec(dec_t.shape, const2),
            pl.BlockSpec(sgu_nw.shape, const2),
            pl.BlockSpec(sgu_w.shape, const3),
            pl.BlockSpec(sgu_bt.shape, const2),
        ],
        out_specs=[out_tok,
                   pl.BlockSpec((1, tm // RET_CHUNK, rw, HEAD_DIM), lambda i, j: (i, j, 0, 0)),
                   out_tok, out_tok, out_tok],
        out_shape=[
            jax.ShapeDtypeStruct((b, s, rw), BF16),
            jax.ShapeDtypeStruct((b, s // RET_CHUNK, rw, HEAD_DIM), F32),
            jax.ShapeDtypeStruct((b, s, rw), F32),
            jax.ShapeDtypeStruct((b, s, rw), BF16),
            jax.ShapeDtypeStruct((b, s, rw), BF16),
        ],
        scratch_shapes=[pltpu.VMEM((RET_HEADS, HEAD_DIM, HEAD_DIM), F32)],
        compiler_params=pltpu.CompilerParams(
            dimension_semantics=("arbitrary", "arbitrary"), vmem_limit_bytes=VMEM_LIMIT_BYTES),
        name="mixer",
    )(x, nw_pre, w_in, cos_t, sin_t, d_t, tab_t, dec_t, sgu_nw, sgu_w, sgu_bt)


def _tail_kernel(x_ref, qb_ref, wb_ref, y_ref, sg_ref, sgu_ref, dec_ref, gnw_ref,
                 nw_ref, wout_ref, wq_ref, kt_ref, vm_ref, wo_ref, wgu_ref, wdown_ref,
                 o_ref, rb_ref):
    d_ff = wdown_ref.shape[0]
    xa_d = wq_ref.shape[1] // XA_HEADS

    @pl.when(pl.program_id(1) == 0)
    def _():
        rb_ref[...] = jnp.zeros_like(rb_ref)

    def chunk_stages(r):
        rw = RET_HEADS * HEAD_DIM
        mix_sgu = _dot(sgu_ref[0, r, :], wout_ref[rw:, :])
        ret = []
        for hd in range(RET_HEADS):
            sl = slice(hd * HEAD_DIM, (hd + 1) * HEAD_DIM)
            y = y_ref[0, r, sl] + _dot(qb_ref[0, r, sl], rb_ref[hd].astype(BF16))
            rb_ref[hd] = (dec_ref[RET_HEADS + hd:RET_HEADS + hd + 1, :] * rb_ref[hd]
                          + wb_ref[0, r.start // RET_CHUNK, sl, :])
            mu = jnp.mean(y, axis=-1, keepdims=True)
            yc = y - mu
            var = jnp.mean(yc * yc, axis=-1, keepdims=True)
            yn = yc * lax.rsqrt(var + EPS) * gnw_ref[:, sl]
            ret.append((sg_ref[0, r, sl].astype(F32) * yn).astype(BF16))
        yield
        mix = _dot(jnp.concatenate(ret, axis=-1), wout_ref[:rw, :]) + mix_sgu
        yield
        x = x_ref[0, r, :] + _rms(mix, nw_ref[0:1, :])
        hq = _rms(x, nw_ref[1:2, :]).astype(BF16)
        yield
        q = _dot(hq, wq_ref[...]) * (xa_d ** -0.5)
        head_cols = [slice(hd * xa_d, (hd + 1) * xa_d) for hd in range(XA_HEADS)]
        scores = [_dot(q[:, sl].astype(BF16), kt_ref[0, sl, :]) for sl in head_cols]
        yield
        heads = []
        for sl, sc in zip(head_cols, scores):
            e = jnp.exp(sc - jnp.max(sc, axis=-1, keepdims=True))
            p = (e * (1.0 / jnp.sum(e, axis=-1, keepdims=True))).astype(BF16)
            heads.append(_dot(p, vm_ref[0, :, sl]).astype(BF16))
        yield
        xa = _dot(jnp.concatenate(heads, axis=-1), wo_ref[...])
        yield
        x = x + _rms(xa, nw_ref[2:3, :])
        hf = _rms(x, nw_ref[3:4, :]).astype(BF16)
        yield
        acts = []
        for lo, hi in FFN_BLOCKS(d_ff):
            g = _dot(hf, wgu_ref[:, lo:hi])
            u = _dot(hf, wgu_ref[:, d_ff + lo:d_ff + hi])
            acts.append((_silu(g) * u).astype(BF16))
            yield
        ff = _dot(jnp.concatenate(acts, axis=-1), wdown_ref[...])
        yield
        o_ref[0, r, :] = x + _rms(ff, nw_ref[4:5, :])

    n_chunks = x_ref.shape[1] // RET_CHUNK
    _run_ordered([chunk_stages(slice(st * RET_CHUNK, (st + 1) * RET_CHUNK))
                  for st in reversed(range(n_chunks))], TAIL_ORDER)


def _tail(x, qb, wb, y, sg, sgu, dec_t, gn_w, nw_tail, w_out, wq, kt, vm, wo, w_gu, w_down):
    b, s, d = x.shape
    tm = TOKEN_TILE
    nt = s // tm
    rw = RET_HEADS * HEAD_DIM
    n_mem = vm.shape[1]
    const2 = lambda i, j: (0, 0)
    rev = lambda i, j: (i, nt - 1 - j, 0)
    per_b = lambda i, j: (i, 0, 0)
    tok_d = pl.BlockSpec((1, tm, d), rev)
    tok_r = pl.BlockSpec((1, tm, rw), rev)
    weight = lambda w: pl.BlockSpec(w.shape, const2, pipeline_mode=pl.Buffered(1))
    return pl.pallas_call(
        _tail_kernel,
        grid=(b, nt),
        in_specs=[
            tok_d, tok_r,
            pl.BlockSpec((1, tm // RET_CHUNK, rw, HEAD_DIM), lambda i, j: (i, nt - 1 - j, 0, 0)),
            tok_r, tok_r, tok_r,
            pl.BlockSpec(dec_t.shape, const2),
            pl.BlockSpec(gn_w.shape, const2),
            pl.BlockSpec(nw_tail.shape, const2),
            weight(w_out), weight(wq),
            pl.BlockSpec((1, d, n_mem), per_b),
            pl.BlockSpec((1, n_mem, d), per_b),
            weight(wo), weight(w_gu), weight(w_down),
        ],
        out_specs=tok_d,
        out_shape=jax.ShapeDtypeStruct((b, s, d), F32),
        scratch_shapes=[pltpu.VMEM((RET_HEADS, HEAD_DIM, HEAD_DIM), F32)],
        compiler_params=pltpu.CompilerParams(
            dimension_semantics=("arbitrary", "arbitrary"), vmem_limit_bytes=VMEM_LIMIT_BYTES),
        name="tail",
    )(x, qb, wb, y, sg, sgu, dec_t, gn_w, nw_tail, w_out, wq, kt, vm, wo, w_gu, w_down)


def _encoder_layer(x, mem, tables, nw, w_in, gn_w, sgu_nw, sgu_w, sgu_bt, w_out, wq, wkv, wo, w_gu, w_down):
    cos_t, sin_t, d_t, tab_t, dec_t = tables
    row = lambda i: nw[i:i + 1, :]
    kt, vm = _mem_kv(mem, row(4), wkv)
    qb, wb, y, sg, sgu = _mixer(x, row(0), w_in, cos_t, sin_t, d_t, tab_t, dec_t, sgu_nw, sgu_w, sgu_bt)
    nw_tail = jnp.concatenate([row(1), row(2), row(3), row(5), row(6)], axis=0)
    return _tail(x, qb, wb, y, sg, sgu, dec_t, gn_w, nw_tail, w_out, wq, kt, vm, wo, w_gu, w_down)


def kernel(x_prompt, x_sample, mem_prompt, mem_sample, norm_w, w_in, ret_log_gamma, ret_gn_w, sgu_norm_w,
           sgu_w, sgu_b, w_out, xa_wq, xa_wkv, xa_wo, ffn_w_gu, ffn_w_down):
    depth = norm_w.shape[0]
    assert w_in.shape[2] == 4 * RET_HEADS * HEAD_DIM + 2 * SGU_GROUPS * SGU_CHUNK
    seq = max(x_prompt.shape[1], x_sample.shape[1])
    cos_t, sin_t = _rope_tables(seq)
    y_prompt, y_sample = x_prompt, x_sample
    for l in range(depth):
        d_t, tab_t, dec_t = _decay_tables(ret_log_gamma[l])
        tables = (cos_t, sin_t, d_t, tab_t, dec_t)
        args = (norm_w[l], w_in[l].astype(BF16), ret_gn_w[l].reshape(1, -1), sgu_norm_w[l].reshape(1, -1),
                sgu_w[l].astype(BF16), jnp.transpose(sgu_b[l]), w_out[l].astype(BF16), xa_wq[l].astype(BF16),
                xa_wkv[l].astype(BF16), xa_wo[l].astype(BF16), ffn_w_gu[l].astype(BF16),
                ffn_w_down[l].astype(BF16))
        y_prompt = _encoder_layer(y_prompt, mem_prompt, tables, *args)
        y_sample = _encoder_layer(y_sample, mem_sample, tables, *args)
    return (y_prompt, y_sample)
```

```python
import math

import jax
import jax.numpy as jnp
from jax import lax
from jax.experimental import pallas as pl
from jax.experimental.pallas import tpu as pltpu

F32 = jnp.float32
BF16 = jnp.bfloat16

EPS = 1e-6
ROPE_BASE = 10000.0
RET_HEADS = 4
HEAD_DIM = 128
SGU_GROUPS = 4
SGU_CHUNK = 128
XA_HEADS = 4
RET_CHUNK = 256
TOKEN_TILE = 512
MIXER_TILE = 1024
VMEM_LIMIT_BYTES = 60 * 1024 * 1024
MXU_WIDTH = 256
TAIL_ORDER = (0, 1, 0, 1, 0, 0, 1, 1, 0, 0, 1, 1, 0, 0, 1, 0, 0, 0, 1, 1, 1, 1, 0, 1, 0, 1)


def _mixer_order(n_chunks):
    order = [0, 0, 0]
    for c in range(n_chunks):
        nxt = [c + 1] if c + 1 < n_chunks else []
        order += nxt + [c, c] + nxt + [c, c] + nxt + [c]
    return order


def _dot(a, b):
    return jnp.dot(a, b, preferred_element_type=F32)


def _rms(x, w):
    return x * lax.rsqrt(jnp.mean(x * x, axis=-1, keepdims=True) + EPS) * w


def _silu(x):
    return x * (1.0 / (1.0 + jnp.exp(-x)))


def FFN_BLOCKS(d_ff):
    step = 3 * MXU_WIDTH
    return [(lo, min(lo + step, d_ff)) for lo in range(0, d_ff, step)]


def _run_ordered(stage_iters, order):
    for i in order:
        next(stage_iters[i], None)
    for it in stage_iters:
        assert next(it, "end") == "end", "stage order does not cover every stage"


def _gelu_tanh(x):
    c = math.sqrt(2.0 / math.pi)
    return x * (0.5 * (1.0 + jnp.tanh(c * (x + 0.044715 * (x * x * x)))))


def _rope_kernel(inv_ref, cos_ref, sin_ref, cos_off_ref, sin_off_ref):
    rows = cos_ref.shape[0]
    half = HEAD_DIM // 2

    @pl.when(pl.program_id(0) == 0)
    def _():
        off = lax.broadcasted_iota(jnp.int32, (rows, HEAD_DIM), 0).astype(F32) * inv_ref[...]
        cos_off_ref[...] = jnp.cos(off)
        sin_off_ref[...] = jnp.sin(off)

    base = (pl.program_id(0) * rows).astype(F32) * inv_ref[...]
    cos_b = jnp.cos(base)
    sin_b = jnp.sin(base)
    lane = lax.broadcasted_iota(jnp.int32, (1, HEAD_DIM), 1)
    sign = jnp.where(lane < half, -1.0, 1.0)
    cos_ref[...] = cos_b * cos_off_ref[...] - sin_b * sin_off_ref[...]
    sin_ref[...] = (sign * sin_b) * cos_off_ref[...] + (sign * cos_b) * sin_off_ref[...]


def _rope_tables(seq):
    half = HEAD_DIM // 2
    inv = ROPE_BASE ** (-jnp.arange(half, dtype=F32) / half)
    inv2 = jnp.concatenate([inv, inv]).reshape(1, HEAD_DIM)
    rows = 512
    assert seq % rows == 0
    return pl.pallas_call(
        _rope_kernel,
        grid=(seq // rows,),
        in_specs=[pl.BlockSpec((1, HEAD_DIM), lambda i: (0, 0))],
        out_specs=[pl.BlockSpec((rows, HEAD_DIM), lambda i: (i, 0))] * 2,
        out_shape=[jax.ShapeDtypeStruct((seq, HEAD_DIM), F32)] * 2,
        scratch_shapes=[pltpu.VMEM((rows, HEAD_DIM), F32)] * 2,
        compiler_params=pltpu.CompilerParams(dimension_semantics=("arbitrary",)),
        name="rope_tables",
    )(inv2)


def _decay_kernel(lg_ref, d_ref, tab_ref, dec_ref):
    c = d_ref.shape[1]
    ii = lax.broadcasted_iota(jnp.int32, (c, c), 0)
    jj = lax.broadcasted_iota(jnp.int32, (c, c), 1)
    dist = (ii - jj).astype(F32)
    adist = jnp.abs(dist)
    idx = lax.broadcasted_iota(jnp.int32, (c, HEAD_DIM), 0).astype(F32)
    ones = jnp.ones((1, HEAD_DIM), F32)
    for h in range(RET_HEADS):
        lf = lg_ref[0, h]
        lb = lg_ref[1, h]
        d_ref[h] = jnp.where(dist >= 0, jnp.exp(lf * adist), jnp.exp(lb * adist))
        sl = slice(h * HEAD_DIM, (h + 1) * HEAD_DIM)
        tab_ref[0, :, sl] = jnp.exp(lf * (idx + 1.0))
        tab_ref[1, :, sl] = jnp.exp(lb * (c - idx))
        tab_ref[2, :, sl] = jnp.exp(lf * (c - 1 - idx))
        tab_ref[3, :, sl] = jnp.exp(lb * idx)
        dec_ref[h:h + 1, :] = jnp.exp((lf * c) * ones)
        dec_ref[RET_HEADS + h:RET_HEADS + h + 1, :] = jnp.exp((lb * c) * ones)


def _decay_tables(log_gamma):
    c = RET_CHUNK
    width = RET_HEADS * HEAD_DIM
    return pl.pallas_call(
        _decay_kernel,
        in_specs=[pl.BlockSpec(memory_space=pltpu.SMEM)],
        out_shape=[
            jax.ShapeDtypeStruct((RET_HEADS, c, c), F32),
            jax.ShapeDtypeStruct((4, c, width), F32),
            jax.ShapeDtypeStruct((2 * RET_HEADS, HEAD_DIM), F32),
        ],
        name="decay_tables",
    )(log_gamma.astype(F32))


def _mem_kernel(mem_ref, nw_ref, wkv_ref, kt_ref, v_ref):
    d = mem_ref.shape[2]
    mn = _rms(mem_ref[0], nw_ref[...]).astype(BF16)
    kv = _dot(mn, wkv_ref[...])
    kt_ref[0] = kv[:, :d].T.astype(BF16)
    v_ref[0] = kv[:, d:].astype(BF16)


def _mem_kv(mem, nw_mem, wkv):
    b, m, d = mem.shape
    return pl.pallas_call(
        _mem_kernel,
        grid=(b,),
        in_specs=[
            pl.BlockSpec((1, m, d), lambda i: (i, 0, 0)),
            pl.BlockSpec((1, d), lambda i: (0, 0)),
            pl.BlockSpec((d, 2 * d), lambda i: (0, 0), pipeline_mode=pl.Buffered(1)),
        ],
        out_specs=[
            pl.BlockSpec((1, d, m), lambda i: (i, 0, 0)),
            pl.BlockSpec((1, m, d), lambda i: (i, 0, 0)),
        ],
        out_shape=[
            jax.ShapeDtypeStruct((b, d, m), BF16),
            jax.ShapeDtypeStruct((b, m, d), BF16),
        ],
        compiler_params=pltpu.CompilerParams(
            dimension_semantics=("arbitrary",), vmem_limit_bytes=VMEM_LIMIT_BYTES),
        name="mem_kv",
    )(mem, nw_mem, wkv)


def _mixer_kernel(x_ref, nw_ref, win_ref, cos_ref, sin_ref, d_ref, tab_ref, dec_ref,
                  sgu_nw_ref, sgu_w_ref, sgu_bt_ref,
                  qb_ref, wb_ref, y_ref, sg_ref, sgu_ref, rf_ref):
    rw = RET_HEADS * HEAD_DIM
    sw = SGU_GROUPS * SGU_CHUNK
    k_scale = HEAD_DIM ** -0.5

    @pl.when(pl.program_id(1) == 0)
    def _():
        rf_ref[...] = jnp.zeros_like(rf_ref)

    def chunk_stages(r):
        h = _rms(x_ref[0, r, :], nw_ref[...]).astype(BF16)
        yield
        zr = _dot(h, win_ref[:, :3 * rw])
        yield
        zg = _dot(h, win_ref[:, 3 * rw:])
        yield
        cos = cos_ref[r, :]
        sin = sin_ref[r, :]
        qs, ks, vs = [], [], []
        for hd in range(RET_HEADS):
            sl = slice(hd * HEAD_DIM, (hd + 1) * HEAD_DIM)
            q = zr[:, hd * HEAD_DIM:(hd + 1) * HEAD_DIM]
            k = zr[:, rw + hd * HEAD_DIM:rw + (hd + 1) * HEAD_DIM]
            v = zr[:, 2 * rw + hd * HEAD_DIM:2 * rw + (hd + 1) * HEAD_DIM]
            q = q * cos + pltpu.roll(q, HEAD_DIM // 2, 1) * sin
            k = (k * cos + pltpu.roll(k, HEAD_DIM // 2, 1) * sin) * k_scale
            k16 = k.astype(BF16)
            qb_ref[0, r, sl] = (q * tab_ref[1, :, sl]).astype(BF16)
            qs.append(q)
            ks.append(k16)
            vs.append(v)
        yield
        scores, y_state = [], []
        for hd in range(RET_HEADS):
            sl = slice(hd * HEAD_DIM, (hd + 1) * HEAD_DIM)
            q, k16, v = qs[hd], ks[hd], vs[hd]
            qf = (q * tab_ref[0, :, sl]).astype(BF16)
            y_state.append(_dot(qf, rf_ref[hd].astype(BF16)))
            vfb = jnp.concatenate([v * tab_ref[2, :, sl], v * tab_ref[3, :, sl]], axis=1).astype(BF16)
            w = lax.dot_general(k16, vfb, (((0,), (0,)), ((), ())), preferred_element_type=F32)
            rf_ref[hd] = dec_ref[hd:hd + 1, :] * rf_ref[hd] + w[:, :HEAD_DIM]
            wb_ref[0, r.start // RET_CHUNK, sl, :] = w[:, HEAD_DIM:]
            scores.append(lax.dot_general(q.astype(BF16), k16, (((1,), (1,)), ((), ())),
                                          preferred_element_type=F32))
        yield
        sg_ref[0, r, :] = _silu(zg[:, :rw]).astype(BF16)
        u = _gelu_tanh(zg[:, rw:rw + sw])
        vg = _gelu_tanh(zg[:, rw + sw:rw + 2 * sw])
        mu = jnp.mean(vg, axis=-1, keepdims=True)
        vc = vg - mu
        var = jnp.mean(vc * vc, axis=-1, keepdims=True)
        vn = (vc * lax.rsqrt(var + EPS) * sgu_nw_ref[...]).astype(BF16)
        yield
        for hd in range(RET_HEADS):
            sl = slice(hd * HEAD_DIM, (hd + 1) * HEAD_DIM)
            p = (scores[hd] * d_ref[hd]).astype(BF16)
            y_ref[0, r, sl] = _dot(p, vs[hd].astype(BF16)) + y_state[hd]
        yield
        n_sub = RET_CHUNK // SGU_CHUNK
        for g in range(SGU_GROUPS):
            cols = slice(g * SGU_CHUNK, (g + 1) * SGU_CHUNK)
            rhs = jnp.concatenate([vn[c * SGU_CHUNK:(c + 1) * SGU_CHUNK, cols] for c in range(n_sub)], axis=1)
            sp = _dot(sgu_w_ref[g], rhs) + sgu_bt_ref[:, g:g + 1]
            for c in range(n_sub):
                rows = slice(c * SGU_CHUNK, (c + 1) * SGU_CHUNK)
                sgu_ref[0, r.start + c * SGU_CHUNK:r.start + (c + 1) * SGU_CHUNK, cols] = (
                    u[rows, cols] * sp[:, rows]).astype(BF16)

    n_chunks = x_ref.shape[1] // RET_CHUNK
    _run_ordered([chunk_stages(slice(st * RET_CHUNK, (st + 1) * RET_CHUNK)) for st in range(n_chunks)],
                 _mixer_order(n_chunks))


def _mixer(x, nw_pre, w_in, cos_t, sin_t, d_t, tab_t, dec_t, sgu_nw, sgu_w, sgu_bt):
    b, s, d = x.shape
    tm = MIXER_TILE
    assert s % tm == 0 and tm % RET_CHUNK == 0
    rw = RET_HEADS * HEAD_DIM
    const2 = lambda i, j: (0, 0)
    const3 = lambda i, j: (0, 0, 0)
    tok = lambda i, j: (i, j, 0)
    out_tok = pl.BlockSpec((1, tm, rw), tok)
    return pl.pallas_call(
        _mixer_kernel,
        grid=(b, s // tm),
        in_specs=[
            pl.BlockSpec((1, tm, d), tok),
            pl.BlockSpec((1, d), const2),
            pl.BlockSpec(w_in.shape, const2, pipeline_mode=pl.Buffered(1)),
            pl.BlockSpec((tm, HEAD_DIM), lambda i, j: (j, 0)),
            pl.BlockSpec((tm, HEAD_DIM), lambda i, j: (j, 0)),
            pl.BlockSpec(d_t.shape, const3, pipeline_mode=pl.Buffered(1)),
            pl.BlockSpec(tab_t.shape, const3, pipeline_mode=pl.Buffered(1)),
            pl.BlockSpec(dec_t.shape, const2),
            pl.BlockSpec(sgu_nw.shape, const2),
            pl.BlockSpec(sgu_w.shape, const3),
            pl.BlockSpec(sgu_bt.shape, const2),
        ],
        out_specs=[out_tok,
                   pl.BlockSpec((1, tm // RET_CHUNK, rw, HEAD_DIM), lambda i, j: (i, j, 0, 0)),
                   out_tok, out_tok, out_tok],
        out_shape=[
            jax.ShapeDtypeStruct((b, s, rw), BF16),
            jax.ShapeDtypeStruct((b, s // RET_CHUNK, rw, HEAD_DIM), F32),
            jax.ShapeDtypeStruct((b, s, rw), F32),
            jax.ShapeDtypeStruct((b, s, rw), BF16),
            jax.ShapeDtypeStruct((b, s, rw), BF16),
        ],
        scratch_shapes=[pltpu.VMEM((RET_HEADS, HEAD_DIM, HEAD_DIM), F32)],
        compiler_params=pltpu.CompilerParams(
            dimension_semantics=("arbitrary", "arbitrary"), vmem_limit_bytes=VMEM_LIMIT_BYTES),
        name="mixer",
    )(x, nw_pre, w_in, cos_t, sin_t, d_t, tab_t, dec_t, sgu_nw, sgu_w, sgu_bt)


def _tail_kernel(x_ref, qb_ref, wb_ref, y_ref, sg_ref, sgu_ref, dec_ref, gnw_ref,
                 nw_ref, wout_ref, wq_ref, kt_ref, vm_ref, wo_ref, wgu_ref, wdown_ref,
                 o_ref, rb_ref):
    d_ff = wdown_ref.shape[0]
    xa_d = wq_ref.shape[1] // XA_HEADS

    @pl.when(pl.program_id(1) == 0)
    def _():
        rb_ref[...] = jnp.zeros_like(rb_ref)

    def chunk_stages(r):
        rw = RET_HEADS * HEAD_DIM
        ret = []
        for hd in range(RET_HEADS):
            sl = slice(hd * HEAD_DIM, (hd + 1) * HEAD_DIM)
            y = y_ref[0, r, sl] + _dot(qb_ref[0, r, sl], rb_ref[hd].astype(BF16))
            rb_ref[hd] = (dec_ref[RET_HEADS + hd:RET_HEADS + hd + 1, :] * rb_ref[hd]
                          + wb_ref[0, r.start // RET_CHUNK, sl, :])
            mu = jnp.mean(y, axis=-1, keepdims=True)
            yc = y - mu
            var = jnp.mean(yc * yc, axis=-1, keepdims=True)
            yn = yc * lax.rsqrt(var + EPS) * gnw_ref[:, sl]
            ret.append((sg_ref[0, r, sl].astype(F32) * yn).astype(BF16))
        yield
        mix_sgu = _dot(sgu_ref[0, r, :], wout_ref[rw:, :])
        mix = _dot(jnp.concatenate(ret, axis=-1), wout_ref[:rw, :]) + mix_sgu
        yield
        x = x_ref[0, r, :] + _rms(mix, nw_ref[0:1, :])
        hq = _rms(x, nw_ref[1:2, :]).astype(BF16)
        yield
        q = _dot(hq, wq_ref[...]) * (xa_d ** -0.5)
        head_cols = [slice(hd * xa_d, (hd + 1) * xa_d) for hd in range(XA_HEADS)]
        scores = [_dot(q[:, sl].astype(BF16), kt_ref[0, sl, :]) for sl in head_cols]
        yield
        heads = []
        for sl, sc in zip(head_cols, scores):
            e = jnp.exp(sc - jnp.max(sc, axis=-1, keepdims=True))
            p = (e * (1.0 / jnp.sum(e, axis=-1, keepdims=True))).astype(BF16)
            heads.append(_dot(p, vm_ref[0, :, sl]).astype(BF16))
        yield
        xa = _dot(jnp.concatenate(heads, axis=-1), wo_ref[...])
        yield
        x = x + _rms(xa, nw_ref[2:3, :])
        hf = _rms(x, nw_ref[3:4, :]).astype(BF16)
        yield
        acts = []
        for lo, hi in FFN_BLOCKS(d_ff):
            g = _dot(hf, wgu_ref[:, lo:hi])
            u = _dot(hf, wgu_ref[:, d_ff + lo:d_ff + hi])
            acts.append((_silu(g) * u).astype(BF16))
            yield
        ff = _dot(jnp.concatenate(acts, axis=-1), wdown_ref[...])
        yield
        o_ref[0, r, :] = x + _rms(ff, nw_ref[4:5, :])

    n_chunks = x_ref.shape[1] // RET_CHUNK
    _run_ordered([chunk_stages(slice(st * RET_CHUNK, (st + 1) * RET_CHUNK))
                  for st in reversed(range(n_chunks))], TAIL_ORDER)


def _tail(x, qb, wb, y, sg, sgu, dec_t, gn_w, nw_tail, w_out, wq, kt, vm, wo, w_gu, w_down):
    b, s, d = x.shape
    tm = TOKEN_TILE
    nt = s // tm
    rw = RET_HEADS * HEAD_DIM
    n_mem = vm.shape[1]
    const2 = lambda i, j: (0, 0)
    rev = lambda i, j: (i, nt - 1 - j, 0)
    per_b = lambda i, j: (i, 0, 0)
    tok_d = pl.BlockSpec((1, tm, d), rev)
    tok_r = pl.BlockSpec((1, tm, rw), rev)
    weight = lambda w: pl.BlockSpec(w.shape, const2, pipeline_mode=pl.Buffered(1))
    return pl.pallas_call(
        _tail_kernel,
        grid=(b, nt),
        in_specs=[
            tok_d, tok_r,
            pl.BlockSpec((1, tm // RET_CHUNK, rw, HEAD_DIM), lambda i, j: (i, nt - 1 - j, 0, 0)),
            tok_r, tok_r, tok_r,
            pl.BlockSpec(dec_t.shape, const2),
            pl.BlockSpec(gn_w.shape, const2),
            pl.BlockSpec(nw_tail.shape, const2),
            weight(w_out), weight(wq),
            pl.BlockSpec((1, d, n_mem), per_b),
            pl.BlockSpec((1, n_mem, d), per_b),
            weight(wo), weight(w_gu), weight(w_down),
        ],
        out_specs=tok_d,
        out_shape=jax.ShapeDtypeStruct((b, s, d), F32),
        scratch_shapes=[pltpu.VMEM((RET_HEADS, HEAD_DIM, HEAD_DIM), F32)],
        compiler_params=pltpu.CompilerParams(
            dimension_semantics=("arbitrary", "arbitrary"), vmem_limit_bytes=VMEM_LIMIT_BYTES),
        name="tail",
    )(x, qb, wb, y, sg, sgu, dec_t, gn_w, nw_tail, w_out, wq, kt, vm, wo, w_gu, w_down)


def _encoder_layer(x, mem, tables, nw, w_in, gn_w, sgu_nw, sgu_w, sgu_bt, w_out, wq, wkv, wo, w_gu, w_down):
    cos_t, sin_t, d_t, tab_t, dec_t = tables
    row = lambda i: nw[i:i + 1, :]
    kt, vm = _mem_kv(mem, row(4), wkv)
    qb, wb, y, sg, sgu = _mixer(x, row(0), w_in, cos_t, sin_t, d_t, tab_t, dec_t, sgu_nw, sgu_w, sgu_bt)
    nw_tail = jnp.concatenate([row(1), row(2), row(3), row(5), row(6)], axis=0)
    return _tail(x, qb, wb, y, sg, sgu, dec_t, gn_w, nw_tail, w_out, wq, kt, vm, wo, w_gu, w_down)


def kernel(x_prompt, x_sample, mem_prompt, mem_sample, norm_w, w_in, ret_log_gamma, ret_gn_w, sgu_norm_w,
           sgu_w, sgu_b, w_out, xa_wq, xa_wkv, xa_wo, ffn_w_gu, ffn_w_down):
    depth = norm_w.shape[0]
    assert w_in.shape[2] == 4 * RET_HEADS * HEAD_DIM + 2 * SGU_GROUPS * SGU_CHUNK
    seq = max(x_prompt.shape[1], x_sample.shape[1])
    cos_t, sin_t = _rope_tables(seq)
    y_prompt, y_sample = x_prompt, x_sample
    for l in range(depth):
        d_t, tab_t, dec_t = _decay_tables(ret_log_gamma[l])
        tables = (cos_t, sin_t, d_t, tab_t, dec_t)
        args = (norm_w[l], w_in[l].astype(BF16), ret_gn_w[l].reshape(1, -1), sgu_norm_w[l].reshape(1, -1),
                sgu_w[l].astype(BF16), jnp.transpose(sgu_b[l]), w_out[l].astype(BF16), xa_wq[l].astype(BF16),
                xa_wkv[l].astype(BF16), xa_wo[l].astype(BF16), ffn_w_gu[l].astype(BF16),
                ffn_w_down[l].astype(BF16))
        y_prompt = _encoder_layer(y_prompt, mem_prompt, tables, *args)
        y_sample = _encoder_layer(y_sample, mem_sample, tables, *args)
    return (y_prompt, y_sample)
```

```python
import math

import jax
import jax.numpy as jnp
from jax import lax
from jax.experimental import pallas as pl
from jax.experimental.pallas import tpu as pltpu

F32 = jnp.float32
BF16 = jnp.bfloat16

EPS = 1e-6
ROPE_BASE = 10000.0
RET_HEADS = 4
HEAD_DIM = 128
SGU_GROUPS = 4
SGU_CHUNK = 128
XA_HEADS = 4
RET_CHUNK = 256
TOKEN_TILE = 512
MIXER_TILE = 1024
VMEM_LIMIT_BYTES = 60 * 1024 * 1024
MXU_WIDTH = 256
TAIL_ORDER = (0, 1, 0, 1, 0, 0, 1, 1, 0, 0, 1, 1, 0, 0, 1, 1)


def _mixer_order(n_chunks):
    order = [0, 0, 0]
    for c in range(n_chunks):
        nxt = [c + 1] if c + 1 < n_chunks else []
        order += nxt + [c, c] + nxt + [c, c] + nxt + [c]
    return order


def _dot(a, b):
    return jnp.dot(a, b, preferred_element_type=F32)


def _pack_rows(w):
    k, n = w.shape
    pairs = jnp.swapaxes(w.astype(BF16).reshape(k // 2, 2, n), 1, 2)
    return lax.bitcast_convert_type(pairs, jnp.uint32)


def _w(ref, rows=None, cols=None):
    r0, r1 = rows if rows is not None else (0, None)
    c0, c1 = cols if cols is not None else (0, None)
    packed = ref[r0 // 2:(None if r1 is None else r1 // 2), c0:c1]
    return pltpu.bitcast(packed, BF16)


def _rms(x, w):
    return x * lax.rsqrt(jnp.mean(x * x, axis=-1, keepdims=True) + EPS) * w


def _silu(x):
    return x * (1.0 / (1.0 + jnp.exp(-x)))


def FFN_BLOCKS(d_ff):
    step = 3 * MXU_WIDTH
    return [(lo, min(lo + step, d_ff)) for lo in range(0, d_ff, step)]


def _run_ordered(stage_iters, order):
    for i in order:
        next(stage_iters[i], None)
    for it in stage_iters:
        assert next(it, "end") == "end", "stage order does not cover every stage"


def _gelu_tanh(x):
    c = math.sqrt(2.0 / math.pi)
    return x * (0.5 * (1.0 + jnp.tanh(c * (x + 0.044715 * (x * x * x)))))


def _rope_kernel(inv_ref, cos_ref, sin_ref, cos_off_ref, sin_off_ref):
    rows = cos_ref.shape[0]
    half = HEAD_DIM // 2

    @pl.when(pl.program_id(0) == 0)
    def _():
        off = lax.broadcasted_iota(jnp.int32, (rows, HEAD_DIM), 0).astype(F32) * inv_ref[...]
        cos_off_ref[...] = jnp.cos(off)
        sin_off_ref[...] = jnp.sin(off)

    base = (pl.program_id(0) * rows).astype(F32) * inv_ref[...]
    cos_b = jnp.cos(base)
    sin_b = jnp.sin(base)
    lane = lax.broadcasted_iota(jnp.int32, (1, HEAD_DIM), 1)
    sign = jnp.where(lane < half, -1.0, 1.0)
    cos_ref[...] = cos_b * cos_off_ref[...] - sin_b * sin_off_ref[...]
    sin_ref[...] = (sign * sin_b) * cos_off_ref[...] + (sign * cos_b) * sin_off_ref[...]


def _rope_tables(seq):
    half = HEAD_DIM // 2
    inv = ROPE_BASE ** (-jnp.arange(half, dtype=F32) / half)
    inv2 = jnp.concatenate([inv, inv]).reshape(1, HEAD_DIM)
    rows = 512
    assert seq % rows == 0
    return pl.pallas_call(
        _rope_kernel,
        grid=(seq // rows,),
        in_specs=[pl.BlockSpec((1, HEAD_DIM), lambda i: (0, 0))],
        out_specs=[pl.BlockSpec((rows, HEAD_DIM), lambda i: (i, 0))] * 2,
        out_shape=[jax.ShapeDtypeStruct((seq, HEAD_DIM), F32)] * 2,
        scratch_shapes=[pltpu.VMEM((rows, HEAD_DIM), F32)] * 2,
        compiler_params=pltpu.CompilerParams(dimension_semantics=("arbitrary",)),
        name="rope_tables",
    )(inv2)


def _decay_kernel(lg_ref, d_ref, tab_ref, dec_ref):
    c = d_ref.shape[1]
    ii = lax.broadcasted_iota(jnp.int32, (c, c), 0)
    jj = lax.broadcasted_iota(jnp.int32, (c, c), 1)
    dist = (ii - jj).astype(F32)
    adist = jnp.abs(dist)
    idx = lax.broadcasted_iota(jnp.int32, (c, HEAD_DIM), 0).astype(F32)
    ones = jnp.ones((1, HEAD_DIM), F32)
    for h in range(RET_HEADS):
        lf = lg_ref[0, h]
        lb = lg_ref[1, h]
        d_ref[h] = jnp.where(dist >= 0, jnp.exp(lf * adist), jnp.exp(lb * adist))
        sl = slice(h * HEAD_DIM, (h + 1) * HEAD_DIM)
        tab_ref[0, :, sl] = jnp.exp(lf * (idx + 1.0))
        tab_ref[1, :, sl] = jnp.exp(lb * (c - idx))
        tab_ref[2, :, sl] = jnp.exp(lf * (c - 1 - idx))
        tab_ref[3, :, sl] = jnp.exp(lb * idx)
        dec_ref[h:h + 1, :] = jnp.exp((lf * c) * ones)
        dec_ref[RET_HEADS + h:RET_HEADS + h + 1, :] = jnp.exp((lb * c) * ones)


def _decay_tables(log_gamma):
    c = RET_CHUNK
    width = RET_HEADS * HEAD_DIM
    return pl.pallas_call(
        _decay_kernel,
        in_specs=[pl.BlockSpec(memory_space=pltpu.SMEM)],
        out_shape=[
            jax.ShapeDtypeStruct((RET_HEADS, c, c), F32),
            jax.ShapeDtypeStruct((4, c, width), F32),
            jax.ShapeDtypeStruct((2 * RET_HEADS, HEAD_DIM), F32),
        ],
        name="decay_tables",
    )(log_gamma.astype(F32))


def _mem_kernel(mem_ref, nw_ref, wkv_ref, kt_ref, v_ref):
    d = mem_ref.shape[2]
    mn = _rms(mem_ref[0], nw_ref[...]).astype(BF16)
    kv = _dot(mn, _w(wkv_ref))
    kt_ref[0] = kv[:, :d].T.astype(BF16)
    v_ref[0] = kv[:, d:].astype(BF16)


def _mem_kv(mem, nw_mem, wkv):
    b, m, d = mem.shape
    return pl.pallas_call(
        _mem_kernel,
        grid=(b,),
        in_specs=[
            pl.BlockSpec((1, m, d), lambda i: (i, 0, 0)),
            pl.BlockSpec((1, d), lambda i: (0, 0)),
            pl.BlockSpec(wkv.shape, lambda i: (0, 0), pipeline_mode=pl.Buffered(1)),
        ],
        out_specs=[
            pl.BlockSpec((1, d, m), lambda i: (i, 0, 0)),
            pl.BlockSpec((1, m, d), lambda i: (i, 0, 0)),
        ],
        out_shape=[
            jax.ShapeDtypeStruct((b, d, m), BF16),
            jax.ShapeDtypeStruct((b, m, d), BF16),
        ],
        compiler_params=pltpu.CompilerParams(
            dimension_semantics=("arbitrary",), vmem_limit_bytes=VMEM_LIMIT_BYTES),
        name="mem_kv",
    )(mem, nw_mem, wkv)


def _mixer_kernel(x_ref, nw_ref, win_ref, cos_ref, sin_ref, d_ref, tab_ref, dec_ref,
                  sgu_nw_ref, sgu_w_ref, sgu_bt_ref,
                  qb_ref, wb_ref, y_ref, sg_ref, sgu_ref, rf_ref):
    rw = RET_HEADS * HEAD_DIM
    sw = SGU_GROUPS * SGU_CHUNK
    k_scale = HEAD_DIM ** -0.5

    @pl.when(pl.program_id(1) == 0)
    def _():
        rf_ref[...] = jnp.zeros_like(rf_ref)

    def chunk_stages(r):
        h = _rms(x_ref[0, r, :], nw_ref[...]).astype(BF16)
        yield
        zr = _dot(h, _w(win_ref, cols=(0, 3 * rw)))
        yield
        zg = _dot(h, _w(win_ref, cols=(3 * rw, None)))
        yield
        cos = cos_ref[r, :]
        sin = sin_ref[r, :]
        qs, ks, vs = [], [], []
        for hd in range(RET_HEADS):
            sl = slice(hd * HEAD_DIM, (hd + 1) * HEAD_DIM)
            q = zr[:, hd * HEAD_DIM:(hd + 1) * HEAD_DIM]
            k = zr[:, rw + hd * HEAD_DIM:rw + (hd + 1) * HEAD_DIM]
            v = zr[:, 2 * rw + hd * HEAD_DIM:2 * rw + (hd + 1) * HEAD_DIM]
            q = q * cos + pltpu.roll(q, HEAD_DIM // 2, 1) * sin
            k = (k * cos + pltpu.roll(k, HEAD_DIM // 2, 1) * sin) * k_scale
            k16 = k.astype(BF16)
            qb_ref[0, r, sl] = (q * tab_ref[1, :, sl]).astype(BF16)
            qs.append(q)
            ks.append(k16)
            vs.append(v)
        yield
        scores, y_state = [], []
        for hd in range(RET_HEADS):
            sl = slice(hd * HEAD_DIM, (hd + 1) * HEAD_DIM)
            q, k16, v = qs[hd], ks[hd], vs[hd]
            qf = (q * tab_ref[0, :, sl]).astype(BF16)
            y_state.append(_dot(qf, rf_ref[hd].astype(BF16)))
            vfb = jnp.concatenate([v * tab_ref[2, :, sl], v * tab_ref[3, :, sl]], axis=1).astype(BF16)
            w = lax.dot_general(k16, vfb, (((0,), (0,)), ((), ())), preferred_element_type=F32)
            rf_ref[hd] = dec_ref[hd:hd + 1, :] * rf_ref[hd] + w[:, :HEAD_DIM]
            wb_ref[0, r.start // RET_CHUNK, sl, :] = w[:, HEAD_DIM:]
            scores.append(lax.dot_general(q.astype(BF16), k16, (((1,), (1,)), ((), ())),
                                          preferred_element_type=F32))
        yield
        sg_ref[0, r, :] = _silu(zg[:, :rw]).astype(BF16)
        u = _gelu_tanh(zg[:, rw:rw + sw])
        vg = _gelu_tanh(zg[:, rw + sw:rw + 2 * sw])
        mu = jnp.mean(vg, axis=-1, keepdims=True)
        vc = vg - mu
        var = jnp.mean(vc * vc, axis=-1, keepdims=True)
        vn = (vc * lax.rsqrt(var + EPS) * sgu_nw_ref[...]).astype(BF16)
        yield
        for hd in range(RET_HEADS):
            sl = slice(hd * HEAD_DIM, (hd + 1) * HEAD_DIM)
            p = (scores[hd] * d_ref[hd]).astype(BF16)
            y_ref[0, r, sl] = _dot(p, vs[hd].astype(BF16)) + y_state[hd]
        yield
        n_sub = RET_CHUNK // SGU_CHUNK
        for g in range(SGU_GROUPS):
            cols = slice(g * SGU_CHUNK, (g + 1) * SGU_CHUNK)
            rhs = jnp.concatenate([vn[c * SGU_CHUNK:(c + 1) * SGU_CHUNK, cols] for c in range(n_sub)], axis=1)
            sp = _dot(sgu_w_ref[g], rhs) + sgu_bt_ref[:, g:g + 1]
            for c in range(n_sub):
                rows = slice(c * SGU_CHUNK, (c + 1) * SGU_CHUNK)
                sgu_ref[0, r.start + c * SGU_CHUNK:r.start + (c + 1) * SGU_CHUNK, cols] = (
                    u[rows, cols] * sp[:, rows]).astype(BF16)

    n_chunks = x_ref.shape[1] // RET_CHUNK
    _run_ordered([chunk_stages(slice(st * RET_CHUNK, (st + 1) * RET_CHUNK)) for st in range(n_chunks)],
                 _mixer_order(n_chunks))


def _mixer(x, nw_pre, w_in, cos_t, sin_t, d_t, tab_t, dec_t, sgu_nw, sgu_w, sgu_bt):
    b, s, d = x.shape
    tm = MIXER_TILE
    assert s % tm == 0 and tm % RET_CHUNK == 0
    rw = RET_HEADS * HEAD_DIM
    const2 = lambda i, j: (0, 0)
    const3 = lambda i, j: (0, 0, 0)
    tok = lambda i, j: (i, j, 0)
    out_tok = pl.BlockSpec((1, tm, rw), tok)
    return pl.pallas_call(
        _mixer_kernel,
        grid=(b, s // tm),
        in_specs=[
            pl.BlockSpec((1, tm, d), tok),
            pl.BlockSpec((1, d), const2),
            pl.BlockSpec(w_in.shape, const2, pipeline_mode=pl.Buffered(1)),
            pl.BlockSpec((tm, HEAD_DIM), lambda i, j: (j, 0)),
            pl.BlockSpec((tm, HEAD_DIM), lambda i, j: (j, 0)),
            pl.BlockSpec(d_t.shape, const3, pipeline_mode=pl.Buffered(1)),
            pl.BlockSpec(tab_t.shape, const3, pipeline_mode=pl.Buffered(1)),
            pl.BlockSpec(dec_t.shape, const2),
            pl.BlockSpec(sgu_nw.shape, const2),
            pl.BlockSpec(sgu_w.shape, const3),
            pl.BlockSpec(sgu_bt.shape, const2),
        ],
        out_specs=[out_tok,
                   pl.BlockSpec((1, tm // RET_CHUNK, rw, HEAD_DIM), lambda i, j: (i, j, 0, 0)),
                   out_tok, out_tok, out_tok],
        out_shape=[
            jax.ShapeDtypeStruct((b, s, rw), BF16),
            jax.ShapeDtypeStruct((b, s // RET_CHUNK, rw, HEAD_DIM), F32),
            jax.ShapeDtypeStruct((b, s, rw), F32),
            jax.ShapeDtypeStruct((b, s, rw), BF16),
            jax.ShapeDtypeStruct((b, s, rw), BF16),
        ],
        scratch_shapes=[pltpu.VMEM((RET_HEADS, HEAD_DIM, HEAD_DIM), F32)],
        compiler_params=pltpu.CompilerParams(
            dimension_semantics=("arbitrary", "arbitrary"), vmem_limit_bytes=VMEM_LIMIT_BYTES),
        name="mixer",
    )(x, nw_pre, w_in, cos_t, sin_t, d_t, tab_t, dec_t, sgu_nw, sgu_w, sgu_bt)


def _tail_kernel(x_ref, qb_ref, wb_ref, y_ref, sg_ref, sgu_ref, dec_ref, gnw_ref,
                 nw_ref, wout_ref, wq_ref, kt_ref, vm_ref, wo_ref, wgu_ref, wdown_ref,
                 o_ref, rb_ref):
    d_ff = wdown_ref.shape[0] * 2
    xa_d = wq_ref.shape[1] // XA_HEADS

    @pl.when(pl.program_id(1) == 0)
    def _():
        rb_ref[...] = jnp.zeros_like(rb_ref)

    def chunk_stages(r):
        rw = RET_HEADS * HEAD_DIM
        ret = []
        for hd in range(RET_HEADS):
            sl = slice(hd * HEAD_DIM, (hd + 1) * HEAD_DIM)
            y = y_ref[0, r, sl] + _dot(qb_ref[0, r, sl], rb_ref[hd].astype(BF16))
            rb_ref[hd] = (dec_ref[RET_HEADS + hd:RET_HEADS + hd + 1, :] * rb_ref[hd]
                          + wb_ref[0, r.start // RET_CHUNK, sl, :])
            mu = jnp.mean(y, axis=-1, keepdims=True)
            yc = y - mu
            var = jnp.mean(yc * yc, axis=-1, keepdims=True)
            yn = yc * lax.rsqrt(var + EPS) * gnw_ref[:, sl]
            ret.append((sg_ref[0, r, sl].astype(F32) * yn).astype(BF16))
        yield
        mix_sgu = _dot(sgu_ref[0, r, :], _w(wout_ref, rows=(rw, None)))
        mix = _dot(jnp.concatenate(ret, axis=-1), _w(wout_ref, rows=(0, rw))) + mix_sgu
        yield
        x = x_ref[0, r, :] + _rms(mix, nw_ref[0:1, :])
        hq = _rms(x, nw_ref[1:2, :]).astype(BF16)
        yield
        q = _dot(hq, _w(wq_ref)) * (xa_d ** -0.5)
        head_cols = [slice(hd * xa_d, (hd + 1) * xa_d) for hd in range(XA_HEADS)]
        scores = [_dot(q[:, sl].astype(BF16), kt_ref[0, sl, :]) for sl in head_cols]
        yield
        heads = []
        for sl, sc in zip(head_cols, scores):
            e = jnp.exp(sc - jnp.max(sc, axis=-1, keepdims=True))
            p = (e * (1.0 / jnp.sum(e, axis=-1, keepdims=True))).astype(BF16)
            heads.append(_dot(p, vm_ref[0, :, sl]).astype(BF16))
        yield
        xa = _dot(jnp.concatenate(heads, axis=-1), _w(wo_ref))
        yield
        x = x + _rms(xa, nw_ref[2:3, :])
        hf = _rms(x, nw_ref[3:4, :]).astype(BF16)
        yield
        act0 = swiglu_block(hf, *blocks[0])
        carried.append((r, x, hf, act0))

    def swiglu_block(hf, lo, hi):
        g = _dot(hf, _w(wgu_ref, cols=(lo, hi)))
        u = _dot(hf, _w(wgu_ref, cols=(d_ff + lo, d_ff + hi)))
        return (_silu(g) * u).astype(BF16)

    blocks = FFN_BLOCKS(d_ff)
    carried = []
    n_chunks = x_ref.shape[1] // RET_CHUNK
    _run_ordered([chunk_stages(slice(st * RET_CHUNK, (st + 1) * RET_CHUNK))
                  for st in reversed(range(n_chunks))], TAIL_ORDER)

    hf_all = jnp.concatenate([c[2] for c in carried], axis=0)
    rest = [swiglu_block(hf_all, lo, hi) for lo, hi in blocks[1:]]
    ffs = []
    for i, (r, x, _, act0) in enumerate(carried):
        rows = slice(i * RET_CHUNK, (i + 1) * RET_CHUNK)
        act = jnp.concatenate([act0] + [a[rows, :] for a in rest], axis=-1)
        ffs.append(_dot(act, _w(wdown_ref)))
    for (r, x, _, _), ff in zip(carried, ffs):
        o_ref[0, r, :] = x + _rms(ff, nw_ref[4:5, :])


def _tail(x, qb, wb, y, sg, sgu, dec_t, gn_w, nw_tail, w_out, wq, kt, vm, wo, w_gu, w_down):
    b, s, d = x.shape
    tm = TOKEN_TILE
    nt = s // tm
    rw = RET_HEADS * HEAD_DIM
    n_mem = vm.shape[1]
    const2 = lambda i, j: (0, 0)
    rev = lambda i, j: (i, nt - 1 - j, 0)
    per_b = lambda i, j: (i, 0, 0)
    tok_d = pl.BlockSpec((1, tm, d), rev)
    tok_r = pl.BlockSpec((1, tm, rw), rev)
    weight = lambda w: pl.BlockSpec(w.shape, const2, pipeline_mode=pl.Buffered(1))
    return pl.pallas_call(
        _tail_kernel,
        grid=(b, nt),
        in_specs=[
            tok_d, tok_r,
            pl.BlockSpec((1, tm // RET_CHUNK, rw, HEAD_DIM), lambda i, j: (i, nt - 1 - j, 0, 0)),
            tok_r, tok_r, tok_r,
            pl.BlockSpec(dec_t.shape, const2),
            pl.BlockSpec(gn_w.shape, const2),
            pl.BlockSpec(nw_tail.shape, const2),
            weight(w_out), weight(wq),
            pl.BlockSpec((1, d, n_mem), per_b),
            pl.BlockSpec((1, n_mem, d), per_b),
            weight(wo), weight(w_gu), weight(w_down),
        ],
        out_specs=tok_d,
        out_shape=jax.ShapeDtypeStruct((b, s, d), F32),
        scratch_shapes=[pltpu.VMEM((RET_HEADS, HEAD_DIM, HEAD_DIM), F32)],
        compiler_params=pltpu.CompilerParams(
            dimension_semantics=("arbitrary", "arbitrary"), vmem_limit_bytes=VMEM_LIMIT_BYTES),
        name="tail",
    )(x, qb, wb, y, sg, sgu, dec_t, gn_w, nw_tail, w_out, wq, kt, vm, wo, w_gu, w_down)


def _encoder_layer(x, mem, tables, nw, w_in, gn_w, sgu_nw, sgu_w, sgu_bt, w_out, wq, wkv, wo, w_gu, w_down):
    cos_t, sin_t, d_t, tab_t, dec_t = tables
    row = lambda i: nw[i:i + 1, :]
    kt, vm = _mem_kv(mem, row(4), wkv)
    qb, wb, y, sg, sgu = _mixer(x, row(0), w_in, cos_t, sin_t, d_t, tab_t, dec_t, sgu_nw, sgu_w, sgu_bt)
    nw_tail = jnp.concatenate([row(1), row(2), row(3), row(5), row(6)], axis=0)
    return _tail(x, qb, wb, y, sg, sgu, dec_t, gn_w, nw_tail, w_out, wq, kt, vm, wo, w_gu, w_down)


def kernel(x_prompt, x_sample, mem_prompt, mem_sample, norm_w, w_in, ret_log_gamma, ret_gn_w, sgu_norm_w,
           sgu_w, sgu_b, w_out, xa_wq, xa_wkv, xa_wo, ffn_w_gu, ffn_w_down):
    depth = norm_w.shape[0]
    assert w_in.shape[2] == 4 * RET_HEADS * HEAD_DIM + 2 * SGU_GROUPS * SGU_CHUNK
    seq = max(x_prompt.shape[1], x_sample.shape[1])
    cos_t, sin_t = _rope_tables(seq)
    y_prompt, y_sample = x_prompt, x_sample
    for l in range(depth):
        d_t, tab_t, dec_t = _decay_tables(ret_log_gamma[l])
        tables = (cos_t, sin_t, d_t, tab_t, dec_t)
        args = (norm_w[l], _pack_rows(w_in[l]), ret_gn_w[l].reshape(1, -1), sgu_norm_w[l].reshape(1, -1),
                sgu_w[l].astype(BF16), jnp.transpose(sgu_b[l]), _pack_rows(w_out[l]), _pack_rows(xa_wq[l]),
                _pack_rows(xa_wkv[l]), _pack_rows(xa_wo[l]), _pack_rows(ffn_w_gu[l]),
                _pack_rows(ffn_w_down[l]))
        y_prompt = _encoder_layer(y_prompt, mem_prompt, tables, *args)
        y_sample = _encoder_layer(y_sample, mem_sample, tables, *args)
    return (y_prompt, y_sample)
```

```python
import math

import jax
import jax.numpy as jnp
from jax import lax
from jax.experimental import pallas as pl
from jax.experimental.pallas import tpu as pltpu

F32 = jnp.float32
BF16 = jnp.bfloat16

EPS = 1e-6
ROPE_BASE = 10000.0
RET_HEADS = 4
HEAD_DIM = 128
SGU_GROUPS = 4
SGU_CHUNK = 128
XA_HEADS = 4
RET_CHUNK = 256
TOKEN_TILE = 512
MIXER_TILE = 1024
VMEM_LIMIT_BYTES = 60 * 1024 * 1024
MXU_WIDTH = 256
TAIL_ORDER = (0, 1, 0, 1, 0, 0, 1, 1, 0, 0, 1, 1, 0, 0, 1, 1)


def _mixer_order(n_chunks):
    order = [0, 0, 0]
    for c in range(n_chunks):
        nxt = [c + 1] if c + 1 < n_chunks else []
        order += nxt + [c, c] + nxt + [c, c] + nxt + [c]
    return order


def _dot(a, b):
    return jnp.dot(a, b, preferred_element_type=F32)


def _pack_rows(w):
    k, n = w.shape
    rows = 256
    assert k % rows == 0

    def pack_kernel(w_ref, o_ref):
        o_ref[...] = pltpu.bitcast(w_ref[...].astype(BF16), jnp.uint32)

    return pl.pallas_call(
        pack_kernel,
        grid=(k // rows,),
        in_specs=[pl.BlockSpec((rows, n), lambda i: (i, 0))],
        out_specs=pl.BlockSpec((rows // 2, n), lambda i: (i, 0)),
        out_shape=jax.ShapeDtypeStruct((k // 2, n), jnp.uint32),
        compiler_params=pltpu.CompilerParams(vmem_limit_bytes=VMEM_LIMIT_BYTES),
        name="pack_weight",
    )(w)


def _w(ref, rows=None, cols=None):
    r0, r1 = rows if rows is not None else (0, None)
    c0, c1 = cols if cols is not None else (0, None)
    packed = ref[r0 // 2:(None if r1 is None else r1 // 2), c0:c1]
    return pltpu.bitcast(packed, BF16)


def _rms(x, w):
    return x * lax.rsqrt(jnp.mean(x * x, axis=-1, keepdims=True) + EPS) * w


def _silu(x):
    return x * (1.0 / (1.0 + jnp.exp(-x)))


def FFN_BLOCKS(d_ff):
    step = 3 * MXU_WIDTH
    return [(lo, min(lo + step, d_ff)) for lo in range(0, d_ff, step)]


def _run_ordered(stage_iters, order):
    for i in order:
        next(stage_iters[i], None)
    for it in stage_iters:
        assert next(it, "end") == "end", "stage order does not cover every stage"


def _gelu_tanh(x):
    c = math.sqrt(2.0 / math.pi)
    return x * (0.5 * (1.0 + jnp.tanh(c * (x + 0.044715 * (x * x * x)))))


def _rope_kernel(inv_ref, cos_ref, sin_ref, cos_off_ref, sin_off_ref):
    rows = cos_ref.shape[0]
    half = HEAD_DIM // 2

    @pl.when(pl.program_id(0) == 0)
    def _():
        off = lax.broadcasted_iota(jnp.int32, (rows, HEAD_DIM), 0).astype(F32) * inv_ref[...]
        cos_off_ref[...] = jnp.cos(off)
        sin_off_ref[...] = jnp.sin(off)

    base = (pl.program_id(0) * rows).astype(F32) * inv_ref[...]
    cos_b = jnp.cos(base)
    sin_b = jnp.sin(base)
    lane = lax.broadcasted_iota(jnp.int32, (1, HEAD_DIM), 1)
    sign = jnp.where(lane < half, -1.0, 1.0)
    cos_ref[...] = cos_b * cos_off_ref[...] - sin_b * sin_off_ref[...]
    sin_ref[...] = (sign * sin_b) * cos_off_ref[...] + (sign * cos_b) * sin_off_ref[...]


def _rope_tables(seq):
    half = HEAD_DIM // 2
    inv = ROPE_BASE ** (-jnp.arange(half, dtype=F32) / half)
    inv2 = jnp.concatenate([inv, inv]).reshape(1, HEAD_DIM)
    rows = 512
    assert seq % rows == 0
    return pl.pallas_call(
        _rope_kernel,
        grid=(seq // rows,),
        in_specs=[pl.BlockSpec((1, HEAD_DIM), lambda i: (0, 0))],
        out_specs=[pl.BlockSpec((rows, HEAD_DIM), lambda i: (i, 0))] * 2,
        out_shape=[jax.ShapeDtypeStruct((seq, HEAD_DIM), F32)] * 2,
        scratch_shapes=[pltpu.VMEM((rows, HEAD_DIM), F32)] * 2,
        compiler_params=pltpu.CompilerParams(dimension_semantics=("arbitrary",)),
        name="rope_tables",
    )(inv2)


def _decay_kernel(lg_ref, d_ref, tab_ref, dec_ref):
    c = d_ref.shape[1]
    ii = lax.broadcasted_iota(jnp.int32, (c, c), 0)
    jj = lax.broadcasted_iota(jnp.int32, (c, c), 1)
    dist = (ii - jj).astype(F32)
    adist = jnp.abs(dist)
    idx = lax.broadcasted_iota(jnp.int32, (c, HEAD_DIM), 0).astype(F32)
    ones = jnp.ones((1, HEAD_DIM), F32)
    for h in range(RET_HEADS):
        lf = lg_ref[0, h]
        lb = lg_ref[1, h]
        d_ref[h] = jnp.where(dist >= 0, jnp.exp(lf * adist), jnp.exp(lb * adist))
        sl = slice(h * HEAD_DIM, (h + 1) * HEAD_DIM)
        tab_ref[0, :, sl] = jnp.exp(lf * (idx + 1.0))
        tab_ref[1, :, sl] = jnp.exp(lb * (c - idx))
        tab_ref[2, :, sl] = jnp.exp(lf * (c - 1 - idx))
        tab_ref[3, :, sl] = jnp.exp(lb * idx)
        dec_ref[h:h + 1, :] = jnp.exp((lf * c) * ones)
        dec_ref[RET_HEADS + h:RET_HEADS + h + 1, :] = jnp.exp((lb * c) * ones)


def _decay_tables(log_gamma):
    c = RET_CHUNK
    width = RET_HEADS * HEAD_DIM
    return pl.pallas_call(
        _decay_kernel,
        in_specs=[pl.BlockSpec(memory_space=pltpu.SMEM)],
        out_shape=[
            jax.ShapeDtypeStruct((RET_HEADS, c, c), F32),
            jax.ShapeDtypeStruct((4, c, width), F32),
            jax.ShapeDtypeStruct((2 * RET_HEADS, HEAD_DIM), F32),
        ],
        name="decay_tables",
    )(log_gamma.astype(F32))


def _mem_kernel(mem_ref, nw_ref, wkv_ref, kt_ref, v_ref):
    d = mem_ref.shape[2]
    mn = _rms(mem_ref[0], nw_ref[...]).astype(BF16)
    kv = _dot(mn, _w(wkv_ref))
    kt_ref[0] = kv[:, :d].T.astype(BF16)
    v_ref[0] = kv[:, d:].astype(BF16)


def _mem_kv(mem, nw_mem, wkv):
    b, m, d = mem.shape
    return pl.pallas_call(
        _mem_kernel,
        grid=(b,),
        in_specs=[
            pl.BlockSpec((1, m, d), lambda i: (i, 0, 0)),
            pl.BlockSpec((1, d), lambda i: (0, 0)),
            pl.BlockSpec(wkv.shape, lambda i: (0, 0), pipeline_mode=pl.Buffered(1)),
        ],
        out_specs=[
            pl.BlockSpec((1, d, m), lambda i: (i, 0, 0)),
            pl.BlockSpec((1, m, d), lambda i: (i, 0, 0)),
        ],
        out_shape=[
            jax.ShapeDtypeStruct((b, d, m), BF16),
            jax.ShapeDtypeStruct((b, m, d), BF16),
        ],
        compiler_params=pltpu.CompilerParams(
            dimension_semantics=("arbitrary",), vmem_limit_bytes=VMEM_LIMIT_BYTES),
        name="mem_kv",
    )(mem, nw_mem, wkv)


def _mixer_kernel(x_ref, nw_ref, win_ref, cos_ref, sin_ref, d_ref, tab_ref, dec_ref,
                  sgu_nw_ref, sgu_w_ref, sgu_bt_ref,
                  qb_ref, wb_ref, y_ref, sg_ref, sgu_ref, rf_ref):
    rw = RET_HEADS * HEAD_DIM
    sw = SGU_GROUPS * SGU_CHUNK
    k_scale = HEAD_DIM ** -0.5

    @pl.when(pl.program_id(1) == 0)
    def _():
        rf_ref[...] = jnp.zeros_like(rf_ref)

    def chunk_stages(r):
        h = _rms(x_ref[0, r, :], nw_ref[...]).astype(BF16)
        yield
        zr = _dot(h, _w(win_ref, cols=(0, 3 * rw)))
        yield
        zg = _dot(h, _w(win_ref, cols=(3 * rw, None)))
        yield
        cos = cos_ref[r, :]
        sin = sin_ref[r, :]
        qs, ks, vs = [], [], []
        for hd in range(RET_HEADS):
            sl = slice(hd * HEAD_DIM, (hd + 1) * HEAD_DIM)
            q = zr[:, hd * HEAD_DIM:(hd + 1) * HEAD_DIM]
            k = zr[:, rw + hd * HEAD_DIM:rw + (hd + 1) * HEAD_DIM]
            v = zr[:, 2 * rw + hd * HEAD_DIM:2 * rw + (hd + 1) * HEAD_DIM]
            q = q * cos + pltpu.roll(q, HEAD_DIM // 2, 1) * sin
            k = (k * cos + pltpu.roll(k, HEAD_DIM // 2, 1) * sin) * k_scale
            k16 = k.astype(BF16)
            qb_ref[0, r, sl] = (q * tab_ref[1, :, sl]).astype(BF16)
            qs.append(q)
            ks.append(k16)
            vs.append(v)
        yield
        scores, y_state = [], []
        for hd in range(RET_HEADS):
            sl = slice(hd * HEAD_DIM, (hd + 1) * HEAD_DIM)
            q, k16, v = qs[hd], ks[hd], vs[hd]
            qf = (q * tab_ref[0, :, sl]).astype(BF16)
            y_state.append(_dot(qf, rf_ref[hd].astype(BF16)))
            vfb = jnp.concatenate([v * tab_ref[2, :, sl], v * tab_ref[3, :, sl]], axis=1).astype(BF16)
            w = lax.dot_general(k16, vfb, (((0,), (0,)), ((), ())), preferred_element_type=F32)
            rf_ref[hd] = dec_ref[hd:hd + 1, :] * rf_ref[hd] + w[:, :HEAD_DIM]
            wb_ref[0, r.start // RET_CHUNK, sl, :] = w[:, HEAD_DIM:]
            scores.append(lax.dot_general(q.astype(BF16), k16, (((1,), (1,)), ((), ())),
                                          preferred_element_type=F32))
        yield
        sg_ref[0, r, :] = _silu(zg[:, :rw]).astype(BF16)
        u = _gelu_tanh(zg[:, rw:rw + sw])
        vg = _gelu_tanh(zg[:, rw + sw:rw + 2 * sw])
        mu = jnp.mean(vg, axis=-1, keepdims=True)
        vc = vg - mu
        var = jnp.mean(vc * vc, axis=-1, keepdims=True)
        vn = (vc * lax.rsqrt(var + EPS) * sgu_nw_ref[...]).astype(BF16)
        yield
        for hd in range(RET_HEADS):
            sl = slice(hd * HEAD_DIM, (hd + 1) * HEAD_DIM)
            p = (scores[hd] * d_ref[hd]).astype(BF16)
            y_ref[0, r, sl] = _dot(p, vs[hd].astype(BF16)) + y_state[hd]
        yield
        n_sub = RET_CHUNK // SGU_CHUNK
        for g in range(SGU_GROUPS):
            cols = slice(g * SGU_CHUNK, (g + 1) * SGU_CHUNK)
            rhs = jnp.concatenate([vn[c * SGU_CHUNK:(c + 1) * SGU_CHUNK, cols] for c in range(n_sub)], axis=1)
            sp = _dot(sgu_w_ref[g], rhs) + sgu_bt_ref[:, g:g + 1]
            for c in range(n_sub):
                rows = slice(c * SGU_CHUNK, (c + 1) * SGU_CHUNK)
                sgu_ref[0, r.start + c * SGU_CHUNK:r.start + (c + 1) * SGU_CHUNK, cols] = (
                    u[rows, cols] * sp[:, rows]).astype(BF16)

    n_chunks = x_ref.shape[1] // RET_CHUNK
    _run_ordered([chunk_stages(slice(st * RET_CHUNK, (st + 1) * RET_CHUNK)) for st in range(n_chunks)],
                 _mixer_order(n_chunks))


def _mixer(x, nw_pre, w_in, cos_t, sin_t, d_t, tab_t, dec_t, sgu_nw, sgu_w, sgu_bt):
    b, s, d = x.shape
    tm = MIXER_TILE
    assert s % tm == 0 and tm % RET_CHUNK == 0
    rw = RET_HEADS * HEAD_DIM
    const2 = lambda i, j: (0, 0)
    const3 = lambda i, j: (0, 0, 0)
    tok = lambda i, j: (i, j, 0)
    out_tok = pl.BlockSpec((1, tm, rw), tok)
    return pl.pallas_call(
        _mixer_kernel,
        grid=(b, s // tm),
        in_specs=[
            pl.BlockSpec((1, tm, d), tok),
            pl.BlockSpec((1, d), const2),
            pl.BlockSpec(w_in.shape, const2, pipeline_mode=pl.Buffered(1)),
            pl.BlockSpec((tm, HEAD_DIM), lambda i, j: (j, 0)),
            pl.BlockSpec((tm, HEAD_DIM), lambda i, j: (j, 0)),
            pl.BlockSpec(d_t.shape, const3, pipeline_mode=pl.Buffered(1)),
            pl.BlockSpec(tab_t.shape, const3, pipeline_mode=pl.Buffered(1)),
            pl.BlockSpec(dec_t.shape, const2),
            pl.BlockSpec(sgu_nw.shape, const2),
            pl.BlockSpec(sgu_w.shape, const3),
            pl.BlockSpec(sgu_bt.shape, const2),
        ],
        out_specs=[out_tok,
                   pl.BlockSpec((1, tm // RET_CHUNK, rw, HEAD_DIM), lambda i, j: (i, j, 0, 0)),
                   out_tok, out_tok, out_tok],
        out_shape=[
            jax.ShapeDtypeStruct((b, s, rw), BF16),
            jax.ShapeDtypeStruct((b, s // RET_CHUNK, rw, HEAD_DIM), F32),
            jax.ShapeDtypeStruct((b, s, rw), F32),
            jax.ShapeDtypeStruct((b, s, rw), BF16),
            jax.ShapeDtypeStruct((b, s, rw), BF16),
        ],
        scratch_shapes=[pltpu.VMEM((RET_HEADS, HEAD_DIM, HEAD_DIM), F32)],
        compiler_params=pltpu.CompilerParams(
            dimension_semantics=("arbitrary", "arbitrary"), vmem_limit_bytes=VMEM_LIMIT_BYTES),
        name="mixer",
    )(x, nw_pre, w_in, cos_t, sin_t, d_t, tab_t, dec_t, sgu_nw, sgu_w, sgu_bt)


def _tail_kernel(x_ref, qb_ref, wb_ref, y_ref, sg_ref, sgu_ref, dec_ref, gnw_ref,
                 nw_ref, wout_ref, wq_ref, kt_ref, vm_ref, wo_ref, wgu_ref, wdown_ref,
                 o_ref, rb_ref):
    d_ff = wdown_ref.shape[0] * 2
    xa_d = wq_ref.shape[1] // XA_HEADS

    @pl.when(pl.program_id(1) == 0)
    def _():
        rb_ref[...] = jnp.zeros_like(rb_ref)

    def chunk_stages(r):
        rw = RET_HEADS * HEAD_DIM
        ret = []
        for hd in range(RET_HEADS):
            sl = slice(hd * HEAD_DIM, (hd + 1) * HEAD_DIM)
            y = y_ref[0, r, sl] + _dot(qb_ref[0, r, sl], rb_ref[hd].astype(BF16))
            rb_ref[hd] = (dec_ref[RET_HEADS + hd:RET_HEADS + hd + 1, :] * rb_ref[hd]
                          + wb_ref[0, r.start // RET_CHUNK, sl, :])
            mu = jnp.mean(y, axis=-1, keepdims=True)
            yc = y - mu
            var = jnp.mean(yc * yc, axis=-1, keepdims=True)
            yn = yc * lax.rsqrt(var + EPS) * gnw_ref[:, sl]
            ret.append((sg_ref[0, r, sl].astype(F32) * yn).astype(BF16))
        yield
        mix_sgu = _dot(sgu_ref[0, r, :], _w(wout_ref, rows=(rw, None)))
        mix = _dot(jnp.concatenate(ret, axis=-1), _w(wout_ref, rows=(0, rw))) + mix_sgu
        yield
        x = x_ref[0, r, :] + _rms(mix, nw_ref[0:1, :])
        hq = _rms(x, nw_ref[1:2, :]).astype(BF16)
        yield
        q = _dot(hq, _w(wq_ref)) * (xa_d ** -0.5)
        head_cols = [slice(hd * xa_d, (hd + 1) * xa_d) for hd in range(XA_HEADS)]
        scores = [_dot(q[:, sl].astype(BF16), kt_ref[0, sl, :]) for sl in head_cols]
        yield
        heads = []
        for sl, sc in zip(head_cols, scores):
            e = jnp.exp(sc - jnp.max(sc, axis=-1, keepdims=True))
            p = (e * (1.0 / jnp.sum(e, axis=-1, keepdims=True))).astype(BF16)
            heads.append(_dot(p, vm_ref[0, :, sl]).astype(BF16))
        yield
        xa = _dot(jnp.concatenate(heads, axis=-1), _w(wo_ref))
        yield
        x = x + _rms(xa, nw_ref[2:3, :])
        hf = _rms(x, nw_ref[3:4, :]).astype(BF16)
        yield
        act0 = swiglu_block(hf, *blocks[0])
        carried.append((r, x, hf, act0))

    def swiglu_block(hf, lo, hi):
        g = _dot(hf, _w(wgu_ref, cols=(lo, hi)))
        u = _dot(hf, _w(wgu_ref, cols=(d_ff + lo, d_ff + hi)))
        return (_silu(g) * u).astype(BF16)

    blocks = FFN_BLOCKS(d_ff)
    carried = []
    n_chunks = x_ref.shape[1] // RET_CHUNK
    _run_ordered([chunk_stages(slice(st * RET_CHUNK, (st + 1) * RET_CHUNK))
                  for st in reversed(range(n_chunks))], TAIL_ORDER)

    hf_all = jnp.concatenate([c[2] for c in carried], axis=0)
    rest = [swiglu_block(hf_all, lo, hi) for lo, hi in blocks[1:]]
    ffs = []
    for i, (r, x, _, act0) in enumerate(carried):
        rows = slice(i * RET_CHUNK, (i + 1) * RET_CHUNK)
        act = jnp.concatenate([act0] + [a[rows, :] for a in rest], axis=-1)
        ffs.append(_dot(act, _w(wdown_ref)))
    for (r, x, _, _), ff in zip(carried, ffs):
        o_ref[0, r, :] = x + _rms(ff, nw_ref[4:5, :])


def _tail(x, qb, wb, y, sg, sgu, dec_t, gn_w, nw_tail, w_out, wq, kt, vm, wo, w_gu, w_down):
    b, s, d = x.shape
    tm = TOKEN_TILE
    nt = s // tm
    rw = RET_HEADS * HEAD_DIM
    n_mem = vm.shape[1]
    const2 = lambda i, j: (0, 0)
    rev = lambda i, j: (i, nt - 1 - j, 0)
    per_b = lambda i, j: (i, 0, 0)
    tok_d = pl.BlockSpec((1, tm, d), rev)
    tok_r = pl.BlockSpec((1, tm, rw), rev)
    weight = lambda w: pl.BlockSpec(w.shape, const2, pipeline_mode=pl.Buffered(1))
    return pl.pallas_call(
        _tail_kernel,
        grid=(b, nt),
        in_specs=[
            tok_d, tok_r,
            pl.BlockSpec((1, tm // RET_CHUNK, rw, HEAD_DIM), lambda i, j: (i, nt - 1 - j, 0, 0)),
            tok_r, tok_r, tok_r,
            pl.BlockSpec(dec_t.shape, const2),
            pl.BlockSpec(gn_w.shape, const2),
            pl.BlockSpec(nw_tail.shape, const2),
            weight(w_out), weight(wq),
            pl.BlockSpec((1, d, n_mem), per_b),
            pl.BlockSpec((1, n_mem, d), per_b),
            weight(wo), weight(w_gu), weight(w_down),
        ],
        out_specs=tok_d,
        out_shape=jax.ShapeDtypeStruct((b, s, d), F32),
        scratch_shapes=[pltpu.VMEM((RET_HEADS, HEAD_DIM, HEAD_DIM), F32)],
        compiler_params=pltpu.CompilerParams(
            dimension_semantics=("arbitrary", "arbitrary"), vmem_limit_bytes=VMEM_LIMIT_BYTES),
        name="tail",
    )(x, qb, wb, y, sg, sgu, dec_t, gn_w, nw_tail, w_out, wq, kt, vm, wo, w_gu, w_down)


def _encoder_layer(x, mem, tables, nw, w_in, gn_w, sgu_nw, sgu_w, sgu_bt, w_out, wq, wkv, wo, w_gu, w_down):
    cos_t, sin_t, d_t, tab_t, dec_t = tables
    row = lambda i: nw[i:i + 1, :]
    kt, vm = _mem_kv(mem, row(4), wkv)
    qb, wb, y, sg, sgu = _mixer(x, row(0), w_in, cos_t, sin_t, d_t, tab_t, dec_t, sgu_nw, sgu_w, sgu_bt)
    nw_tail = jnp.concatenate([row(1), row(2), row(3), row(5), row(6)], axis=0)
    return _tail(x, qb, wb, y, sg, sgu, dec_t, gn_w, nw_tail, w_out, wq, kt, vm, wo, w_gu, w_down)


def kernel(x_prompt, x_sample, mem_prompt, mem_sample, norm_w, w_in, ret_log_gamma, ret_gn_w, sgu_norm_w,
           sgu_w, sgu_b, w_out, xa_wq, xa_wkv, xa_wo, ffn_w_gu, ffn_w_down):
    depth = norm_w.shape[0]
    assert w_in.shape[2] == 4 * RET_HEADS * HEAD_DIM + 2 * SGU_GROUPS * SGU_CHUNK
    seq = max(x_prompt.shape[1], x_sample.shape[1])
    cos_t, sin_t = _rope_tables(seq)
    y_prompt, y_sample = x_prompt, x_sample
    for l in range(depth):
        d_t, tab_t, dec_t = _decay_tables(ret_log_gamma[l])
        tables = (cos_t, sin_t, d_t, tab_t, dec_t)
        args = (norm_w[l], _pack_rows(w_in[l]), ret_gn_w[l].reshape(1, -1), sgu_norm_w[l].reshape(1, -1),
                sgu_w[l].astype(BF16), jnp.transpose(sgu_b[l]), _pack_rows(w_out[l]), _pack_rows(xa_wq[l]),
                _pack_rows(xa_wkv[l]), _pack_rows(xa_wo[l]), _pack_rows(ffn_w_gu[l]),
                _pack_rows(ffn_w_down[l]))
        y_prompt = _encoder_layer(y_prompt, mem_prompt, tables, *args)
        y_sample = _encoder_layer(y_sample, mem_sample, tables, *args)
    return (y_prompt, y_sample)
```

```python
import math

import jax
import jax.numpy as jnp
from jax import lax
from jax.experimental import pallas as pl
from jax.experimental.pallas import tpu as pltpu

F32 = jnp.float32
BF16 = jnp.bfloat16

EPS = 1e-6
ROPE_BASE = 10000.0
RET_HEADS = 4
HEAD_DIM = 128
SGU_GROUPS = 4
SGU_CHUNK = 128
XA_HEADS = 4
RET_CHUNK = 256
TOKEN_TILE = 512
MIXER_TILE = 1024
VMEM_LIMIT_BYTES = 60 * 1024 * 1024
MXU_WIDTH = 256
TAIL_ORDER = (0, 1, 0, 1, 0, 0, 1, 1, 0, 0, 1, 1, 0, 0, 1, 1)


def _mixer_order(n_chunks):
    order = [0, 0, 0]
    for c in range(n_chunks):
        nxt = [c + 1] if c + 1 < n_chunks else []
        order += nxt + [c, c] + nxt + [c, c] + nxt + [c]
    return order


def _dot(a, b):
    return jnp.dot(a, b, preferred_element_type=F32)


def _pack_rows(w):
    k, n = w.shape
    rows = 256
    assert k % rows == 0

    def pack_kernel(w_ref, o_ref):
        o_ref[...] = pltpu.bitcast(w_ref[...].astype(BF16), jnp.uint32)

    return pl.pallas_call(
        pack_kernel,
        grid=(k // rows,),
        in_specs=[pl.BlockSpec((rows, n), lambda i: (i, 0))],
        out_specs=pl.BlockSpec((rows // 2, n), lambda i: (i, 0)),
        out_shape=jax.ShapeDtypeStruct((k // 2, n), jnp.uint32),
        compiler_params=pltpu.CompilerParams(vmem_limit_bytes=VMEM_LIMIT_BYTES),
        name="pack_weight",
    )(w)


def _pack(x):
    return pltpu.bitcast(x.astype(BF16), jnp.uint32)


def _unpack(packed):
    return pltpu.bitcast(packed, BF16)


def _half(rows):
    return slice(rows.start // 2, rows.stop // 2)


def _w(ref, rows=None, cols=None):
    r0, r1 = rows if rows is not None else (0, None)
    c0, c1 = cols if cols is not None else (0, None)
    packed = ref[r0 // 2:(None if r1 is None else r1 // 2), c0:c1]
    return pltpu.bitcast(packed, BF16)


def _rms(x, w):
    return x * lax.rsqrt(jnp.mean(x * x, axis=-1, keepdims=True) + EPS) * w


def _silu(x):
    return x * (1.0 / (1.0 + jnp.exp(-x)))


def _ffn_blocks(d_ff):
    step = 3 * MXU_WIDTH
    return [(lo, min(lo + step, d_ff)) for lo in range(0, d_ff, step)]


def _run_ordered(stage_iters, order):
    for i in order:
        next(stage_iters[i], None)
    for it in stage_iters:
        assert next(it, "end") == "end", "stage order does not cover every stage"


def _gelu_tanh(x):
    c = math.sqrt(2.0 / math.pi)
    return x * (0.5 * (1.0 + jnp.tanh(c * (x + 0.044715 * (x * x * x)))))


def _rope_kernel(inv_ref, cos_ref, sin_ref, cos_off_ref, sin_off_ref):
    rows = cos_ref.shape[0]
    half = HEAD_DIM // 2

    @pl.when(pl.program_id(0) == 0)
    def _():
        off = lax.broadcasted_iota(jnp.int32, (rows, HEAD_DIM), 0).astype(F32) * inv_ref[...]
        cos_off_ref[...] = jnp.cos(off)
        sin_off_ref[...] = jnp.sin(off)

    base = (pl.program_id(0) * rows).astype(F32) * inv_ref[...]
    cos_b = jnp.cos(base)
    sin_b = jnp.sin(base)
    lane = lax.broadcasted_iota(jnp.int32, (1, HEAD_DIM), 1)
    sign = jnp.where(lane < half, -1.0, 1.0)
    cos_ref[...] = cos_b * cos_off_ref[...] - sin_b * sin_off_ref[...]
    sin_ref[...] = (sign * sin_b) * cos_off_ref[...] + (sign * cos_b) * sin_off_ref[...]


def _rope_tables(seq):
    half = HEAD_DIM // 2
    inv = ROPE_BASE ** (-jnp.arange(half, dtype=F32) / half)
    inv2 = jnp.concatenate([inv, inv]).reshape(1, HEAD_DIM)
    rows = 512
    assert seq % rows == 0
    return pl.pallas_call(
        _rope_kernel,
        grid=(seq // rows,),
        in_specs=[pl.BlockSpec((1, HEAD_DIM), lambda i: (0, 0))],
        out_specs=[pl.BlockSpec((rows, HEAD_DIM), lambda i: (i, 0))] * 2,
        out_shape=[jax.ShapeDtypeStruct((seq, HEAD_DIM), F32)] * 2,
        scratch_shapes=[pltpu.VMEM((rows, HEAD_DIM), F32)] * 2,
        compiler_params=pltpu.CompilerParams(dimension_semantics=("arbitrary",)),
        name="rope_tables",
    )(inv2)


def _decay_kernel(lg_ref, d_ref, tab_ref, dec_ref):
    c = d_ref.shape[1]
    ii = lax.broadcasted_iota(jnp.int32, (c, c), 0)
    jj = lax.broadcasted_iota(jnp.int32, (c, c), 1)
    dist = (ii - jj).astype(F32)
    adist = jnp.abs(dist)
    idx = lax.broadcasted_iota(jnp.int32, (c, HEAD_DIM), 0).astype(F32)
    ones = jnp.ones((1, HEAD_DIM), F32)
    for h in range(RET_HEADS):
        lf = lg_ref[0, h]
        lb = lg_ref[1, h]
        d_ref[h] = jnp.where(dist >= 0, jnp.exp(lf * adist), jnp.exp(lb * adist))
        sl = slice(h * HEAD_DIM, (h + 1) * HEAD_DIM)
        tab_ref[0, :, sl] = jnp.exp(lf * (idx + 1.0))
        tab_ref[1, :, sl] = jnp.exp(lb * (c - idx))
        tab_ref[2, :, sl] = jnp.exp(lf * (c - 1 - idx))
        tab_ref[3, :, sl] = jnp.exp(lb * idx)
        dec_ref[h:h + 1, :] = jnp.exp((lf * c) * ones)
        dec_ref[RET_HEADS + h:RET_HEADS + h + 1, :] = jnp.exp((lb * c) * ones)


def _decay_tables(log_gamma):
    c = RET_CHUNK
    width = RET_HEADS * HEAD_DIM
    return pl.pallas_call(
        _decay_kernel,
        in_specs=[pl.BlockSpec(memory_space=pltpu.SMEM)],
        out_shape=[
            jax.ShapeDtypeStruct((RET_HEADS, c, c), F32),
            jax.ShapeDtypeStruct((4, c, width), F32),
            jax.ShapeDtypeStruct((2 * RET_HEADS, HEAD_DIM), F32),
        ],
        name="decay_tables",
    )(log_gamma.astype(F32))


def _mem_kernel(mem_ref, nw_ref, wkv_ref, kt_ref, v_ref):
    d = mem_ref.shape[2]
    mn = _rms(mem_ref[0], nw_ref[...]).astype(BF16)
    kv = _dot(mn, _w(wkv_ref))
    kt_ref[0] = _pack(kv[:, :d].T)
    v_ref[0] = _pack(kv[:, d:])


def _mem_kv(mem, nw_mem, wkv):
    b, m, d = mem.shape
    return pl.pallas_call(
        _mem_kernel,
        grid=(b,),
        in_specs=[
            pl.BlockSpec((1, m, d), lambda i: (i, 0, 0)),
            pl.BlockSpec((1, d), lambda i: (0, 0)),
            pl.BlockSpec(wkv.shape, lambda i: (0, 0), pipeline_mode=pl.Buffered(1)),
        ],
        out_specs=[
            pl.BlockSpec((1, d // 2, m), lambda i: (i, 0, 0)),
            pl.BlockSpec((1, m // 2, d), lambda i: (i, 0, 0)),
        ],
        out_shape=[
            jax.ShapeDtypeStruct((b, d // 2, m), jnp.uint32),
            jax.ShapeDtypeStruct((b, m // 2, d), jnp.uint32),
        ],
        compiler_params=pltpu.CompilerParams(
            dimension_semantics=("arbitrary",), vmem_limit_bytes=VMEM_LIMIT_BYTES),
        name="mem_kv",
    )(mem, nw_mem, wkv)


def _mixer_kernel(x_ref, nw_ref, win_ref, cos_ref, sin_ref, d_ref, tab_ref, dec_ref,
                  sgu_nw_ref, sgu_w_ref, sgu_bt_ref,
                  qb_ref, wb_ref, y_ref, sg_ref, sgu_ref, rf_ref):
    rw = RET_HEADS * HEAD_DIM
    sw = SGU_GROUPS * SGU_CHUNK
    k_scale = HEAD_DIM ** -0.5

    @pl.when(pl.program_id(1) == 0)
    def _():
        rf_ref[...] = jnp.zeros_like(rf_ref)

    def chunk_stages(r):
        h = _rms(x_ref[0, r, :], nw_ref[...]).astype(BF16)
        yield
        zr = _dot(h, _w(win_ref, cols=(0, 3 * rw)))
        yield
        zg = _dot(h, _w(win_ref, cols=(3 * rw, None)))
        yield
        cos = cos_ref[r, :]
        sin = sin_ref[r, :]
        qs, ks, vs = [], [], []
        for hd in range(RET_HEADS):
            sl = slice(hd * HEAD_DIM, (hd + 1) * HEAD_DIM)
            q = zr[:, hd * HEAD_DIM:(hd + 1) * HEAD_DIM]
            k = zr[:, rw + hd * HEAD_DIM:rw + (hd + 1) * HEAD_DIM]
            v = zr[:, 2 * rw + hd * HEAD_DIM:2 * rw + (hd + 1) * HEAD_DIM]
            q = q * cos + pltpu.roll(q, HEAD_DIM // 2, 1) * sin
            k = (k * cos + pltpu.roll(k, HEAD_DIM // 2, 1) * sin) * k_scale
            k16 = k.astype(BF16)
            qb_ref[0, _half(r), sl] = _pack(q * tab_ref[1, :, sl])
            qs.append(q)
            ks.append(k16)
            vs.append(v)
        yield
        scores, y_state = [], []
        for hd in range(RET_HEADS):
            sl = slice(hd * HEAD_DIM, (hd + 1) * HEAD_DIM)
            q, k16, v = qs[hd], ks[hd], vs[hd]
            qf = (q * tab_ref[0, :, sl]).astype(BF16)
            y_state.append(_dot(qf, rf_ref[hd].astype(BF16)))
            vfb = jnp.concatenate([v * tab_ref[2, :, sl], v * tab_ref[3, :, sl]], axis=1).astype(BF16)
            w = lax.dot_general(k16, vfb, (((0,), (0,)), ((), ())), preferred_element_type=F32)
            rf_ref[hd] = dec_ref[hd:hd + 1, :] * rf_ref[hd] + w[:, :HEAD_DIM]
            wb_ref[0, r.start // RET_CHUNK, sl, :] = w[:, HEAD_DIM:]
            scores.append(lax.dot_general(q.astype(BF16), k16, (((1,), (1,)), ((), ())),
                                          preferred_element_type=F32))
        yield
        sg_ref[0, _half(r), :] = _pack(_silu(zg[:, :rw]))
        u = _gelu_tanh(zg[:, rw:rw + sw])
        vg = _gelu_tanh(zg[:, rw + sw:rw + 2 * sw])
        mu = jnp.mean(vg, axis=-1, keepdims=True)
        vc = vg - mu
        var = jnp.mean(vc * vc, axis=-1, keepdims=True)
        vn = (vc * lax.rsqrt(var + EPS) * sgu_nw_ref[...]).astype(BF16)
        yield
        for hd in range(RET_HEADS):
            sl = slice(hd * HEAD_DIM, (hd + 1) * HEAD_DIM)
            p = (scores[hd] * d_ref[hd]).astype(BF16)
            y_ref[0, r, sl] = _dot(p, vs[hd].astype(BF16)) + y_state[hd]
        yield
        n_sub = RET_CHUNK // SGU_CHUNK
        for g in range(SGU_GROUPS):
            cols = slice(g * SGU_CHUNK, (g + 1) * SGU_CHUNK)
            rhs = jnp.concatenate([vn[c * SGU_CHUNK:(c + 1) * SGU_CHUNK, cols] for c in range(n_sub)], axis=1)
            sp = _dot(sgu_w_ref[g], rhs) + sgu_bt_ref[:, g:g + 1]
            for c in range(n_sub):
                rows = slice(c * SGU_CHUNK, (c + 1) * SGU_CHUNK)
                out_rows = slice(r.start + c * SGU_CHUNK, r.start + (c + 1) * SGU_CHUNK)
                sgu_ref[0, _half(out_rows), cols] = _pack(u[rows, cols] * sp[:, rows])

    n_chunks = x_ref.shape[1] // RET_CHUNK
    _run_ordered([chunk_stages(slice(st * RET_CHUNK, (st + 1) * RET_CHUNK)) for st in range(n_chunks)],
                 _mixer_order(n_chunks))


def _mixer(x, nw_pre, w_in, cos_t, sin_t, d_t, tab_t, dec_t, sgu_nw, sgu_w, sgu_bt):
    b, s, d = x.shape
    tm = MIXER_TILE
    assert s % tm == 0 and tm % RET_CHUNK == 0
    rw = RET_HEADS * HEAD_DIM
    const2 = lambda i, j: (0, 0)
    const3 = lambda i, j: (0, 0, 0)
    tok = lambda i, j: (i, j, 0)
    out_tok = pl.BlockSpec((1, tm, rw), tok)
    out_packed = pl.BlockSpec((1, tm // 2, rw), tok)
    return pl.pallas_call(
        _mixer_kernel,
        grid=(b, s // tm),
        in_specs=[
            pl.BlockSpec((1, tm, d), tok),
            pl.BlockSpec((1, d), const2),
            pl.BlockSpec(w_in.shape, const2, pipeline_mode=pl.Buffered(1)),
            pl.BlockSpec((tm, HEAD_DIM), lambda i, j: (j, 0)),
            pl.BlockSpec((tm, HEAD_DIM), lambda i, j: (j, 0)),
            pl.BlockSpec(d_t.shape, const3, pipeline_mode=pl.Buffered(1)),
            pl.BlockSpec(tab_t.shape, const3, pipeline_mode=pl.Buffered(1)),
            pl.BlockSpec(dec_t.shape, const2),
            pl.BlockSpec(sgu_nw.shape, const2),
            pl.BlockSpec(sgu_w.shape, const3),
            pl.BlockSpec(sgu_bt.shape, const2),
        ],
        out_specs=[out_packed,
                   pl.BlockSpec((1, tm // RET_CHUNK, rw, HEAD_DIM), lambda i, j: (i, j, 0, 0)),
                   out_tok, out_packed, out_packed],
        out_shape=[
            jax.ShapeDtypeStruct((b, s // 2, rw), jnp.uint32),
            jax.ShapeDtypeStruct((b, s // RET_CHUNK, rw, HEAD_DIM), F32),
            jax.ShapeDtypeStruct((b, s, rw), F32),
            jax.ShapeDtypeStruct((b, s // 2, rw), jnp.uint32),
            jax.ShapeDtypeStruct((b, s // 2, rw), jnp.uint32),
        ],
        scratch_shapes=[pltpu.VMEM((RET_HEADS, HEAD_DIM, HEAD_DIM), F32)],
        compiler_params=pltpu.CompilerParams(
            dimension_semantics=("arbitrary", "arbitrary"), vmem_limit_bytes=VMEM_LIMIT_BYTES),
        name="mixer",
    )(x, nw_pre, w_in, cos_t, sin_t, d_t, tab_t, dec_t, sgu_nw, sgu_w, sgu_bt)


def _tail_kernel(x_ref, qb_ref, wb_ref, y_ref, sg_ref, sgu_ref, dec_ref, gnw_ref,
                 nw_ref, wout_ref, wq_ref, kt_ref, vm_ref, wo_ref, wgu_ref, wdown_ref,
                 o_ref, rb_ref):
    d_ff = wdown_ref.shape[0] * 2
    xa_d = wq_ref.shape[1] // XA_HEADS

    @pl.when(pl.program_id(1) == 0)
    def _():
        rb_ref[...] = jnp.zeros_like(rb_ref)

    def chunk_stages(r):
        rw = RET_HEADS * HEAD_DIM
        ret = []
        for hd in range(RET_HEADS):
            sl = slice(hd * HEAD_DIM, (hd + 1) * HEAD_DIM)
            y = y_ref[0, r, sl] + _dot(_unpack(qb_ref[0, _half(r), sl]), rb_ref[hd].astype(BF16))
            rb_ref[hd] = (dec_ref[RET_HEADS + hd:RET_HEADS + hd + 1, :] * rb_ref[hd]
                          + wb_ref[0, r.start // RET_CHUNK, sl, :])
            mu = jnp.mean(y, axis=-1, keepdims=True)
            yc = y - mu
            var = jnp.mean(yc * yc, axis=-1, keepdims=True)
            yn = yc * lax.rsqrt(var + EPS) * gnw_ref[:, sl]
            ret.append((_unpack(sg_ref[0, _half(r), sl]).astype(F32) * yn).astype(BF16))
        yield
        mix_sgu = _dot(_unpack(sgu_ref[0, _half(r), :]), _w(wout_ref, rows=(rw, None)))
        mix = _dot(jnp.concatenate(ret, axis=-1), _w(wout_ref, rows=(0, rw))) + mix_sgu
        yield
        x = x_ref[0, r, :] + _rms(mix, nw_ref[0:1, :])
        hq = _rms(x, nw_ref[1:2, :]).astype(BF16)
        yield
        q = _dot(hq, _w(wq_ref)) * (xa_d ** -0.5)
        head_cols = [slice(hd * xa_d, (hd + 1) * xa_d) for hd in range(XA_HEADS)]
        scores = [_dot(q[:, sl].astype(BF16), _unpack(kt_ref[0, _half(sl), :])) for sl in head_cols]
        yield
        heads = []
        for sl, sc in zip(head_cols, scores):
            e = jnp.exp(sc - jnp.max(sc, axis=-1, keepdims=True))
            p = (e * (1.0 / jnp.sum(e, axis=-1, keepdims=True))).astype(BF16)
            heads.append(_dot(p, _unpack(vm_ref[0, :, sl])).astype(BF16))
        yield
        xa = _dot(jnp.concatenate(heads, axis=-1), _w(wo_ref))
        yield
        x = x + _rms(xa, nw_ref[2:3, :])
        hf = _rms(x, nw_ref[3:4, :]).astype(BF16)
        yield
        act0 = swiglu_block(hf, *blocks[0])
        carried.append((r, x, hf, act0))

    def swiglu_block(hf, lo, hi):
        g = _dot(hf, _w(wgu_ref, cols=(lo, hi)))
        u = _dot(hf, _w(wgu_ref, cols=(d_ff + lo, d_ff + hi)))
        return (_silu(g) * u).astype(BF16)

    blocks = _ffn_blocks(d_ff)
    carried = []
    n_chunks = x_ref.shape[1] // RET_CHUNK
    _run_ordered([chunk_stages(slice(st * RET_CHUNK, (st + 1) * RET_CHUNK))
                  for st in reversed(range(n_chunks))], TAIL_ORDER)

    hf_all = jnp.concatenate([c[2] for c in carried], axis=0)
    rest = [swiglu_block(hf_all, lo, hi) for lo, hi in blocks[1:]]
    ffs = []
    for i, (r, x, _, act0) in enumerate(carried):
        rows = slice(i * RET_CHUNK, (i + 1) * RET_CHUNK)
        act = jnp.concatenate([act0] + [a[rows, :] for a in rest], axis=-1)
        ffs.append(_dot(act, _w(wdown_ref)))
    for (r, x, _, _), ff in zip(carried, ffs):
        o_ref[0, r, :] = x + _rms(ff, nw_ref[4:5, :])


def _tail(x, qb, wb, y, sg, sgu, dec_t, gn_w, nw_tail, w_out, wq, kt, vm, wo, w_gu, w_down):
    b, s, d = x.shape
    tm = TOKEN_TILE
    nt = s // tm
    rw = RET_HEADS * HEAD_DIM
    n_mem = vm.shape[1] * 2
    const2 = lambda i, j: (0, 0)
    rev = lambda i, j: (i, nt - 1 - j, 0)
    per_b = lambda i, j: (i, 0, 0)
    tok_d = pl.BlockSpec((1, tm, d), rev)
    tok_r = pl.BlockSpec((1, tm, rw), rev)
    tok_p = pl.BlockSpec((1, tm // 2, rw), rev)
    weight = lambda w: pl.BlockSpec(w.shape, const2, pipeline_mode=pl.Buffered(1))
    return pl.pallas_call(
        _tail_kernel,
        grid=(b, nt),
        in_specs=[
            tok_d, tok_p,
            pl.BlockSpec((1, tm // RET_CHUNK, rw, HEAD_DIM), lambda i, j: (i, nt - 1 - j, 0, 0)),
            tok_r, tok_p, tok_p,
            pl.BlockSpec(dec_t.shape, const2),
            pl.BlockSpec(gn_w.shape, const2),
            pl.BlockSpec(nw_tail.shape, const2),
            weight(w_out), weight(wq),
            pl.BlockSpec((1, d // 2, n_mem), per_b),
            pl.BlockSpec((1, n_mem // 2, d), per_b),
            weight(wo), weight(w_gu), weight(w_down),
        ],
        out_specs=tok_d,
        out_shape=jax.ShapeDtypeStruct((b, s, d), F32),
        scratch_shapes=[pltpu.VMEM((RET_HEADS, HEAD_DIM, HEAD_DIM), F32)],
        compiler_params=pltpu.CompilerParams(
            dimension_semantics=("arbitrary", "arbitrary"), vmem_limit_bytes=VMEM_LIMIT_BYTES),
        name="tail",
    )(x, qb, wb, y, sg, sgu, dec_t, gn_w, nw_tail, w_out, wq, kt, vm, wo, w_gu, w_down)


def _encoder_layer(x, mem, tables, nw, w_in, gn_w, sgu_nw, sgu_w, sgu_bt, w_out, wq, wkv, wo, w_gu, w_down):
    cos_t, sin_t, d_t, tab_t, dec_t = tables
    row = lambda i: nw[i:i + 1, :]
    kt, vm = _mem_kv(mem, row(4), wkv)
    qb, wb, y, sg, sgu = _mixer(x, row(0), w_in, cos_t, sin_t, d_t, tab_t, dec_t, sgu_nw, sgu_w, sgu_bt)
    nw_tail = jnp.concatenate([row(1), row(2), row(3), row(5), row(6)], axis=0)
    return _tail(x, qb, wb, y, sg, sgu, dec_t, gn_w, nw_tail, w_out, wq, kt, vm, wo, w_gu, w_down)


def kernel(x_prompt, x_sample, mem_prompt, mem_sample, norm_w, w_in, ret_log_gamma, ret_gn_w, sgu_norm_w,
           sgu_w, sgu_b, w_out, xa_wq, xa_wkv, xa_wo, ffn_w_gu, ffn_w_down):
    depth = norm_w.shape[0]
    assert w_in.shape[2] == 4 * RET_HEADS * HEAD_DIM + 2 * SGU_GROUPS * SGU_CHUNK
    seq = max(x_prompt.shape[1], x_sample.shape[1])
    cos_t, sin_t = _rope_tables(seq)
    y_prompt, y_sample = x_prompt, x_sample
    for l in range(depth):
        d_t, tab_t, dec_t = _decay_tables(ret_log_gamma[l])
        tables = (cos_t, sin_t, d_t, tab_t, dec_t)
        args = (norm_w[l], _pack_rows(w_in[l]), ret_gn_w[l].reshape(1, -1), sgu_norm_w[l].reshape(1, -1),
                sgu_w[l].astype(BF16), jnp.transpose(sgu_b[l]), _pack_rows(w_out[l]), _pack_rows(xa_wq[l]),
                _pack_rows(xa_wkv[l]), _pack_rows(xa_wo[l]), _pack_rows(ffn_w_gu[l]),
                _pack_rows(ffn_w_down[l]))
        y_prompt = _encoder_layer(y_prompt, mem_prompt, tables, *args)
        y_sample = _encoder_layer(y_sample, mem_sample, tables, *args)
    return (y_prompt, y_sample)
```

```python
import math

import jax
import jax.numpy as jnp
from jax import lax
from jax.experimental import pallas as pl
from jax.experimental.pallas import tpu as pltpu

F32 = jnp.float32
BF16 = jnp.bfloat16

EPS = 1e-6
ROPE_BASE = 10000.0
RET_HEADS = 4
HEAD_DIM = 128
SGU_GROUPS = 4
SGU_CHUNK = 128
XA_HEADS = 4
RET_CHUNK = 256
TOKEN_TILE = 512
MIXER_TILE = 1024
VMEM_LIMIT_BYTES = 60 * 1024 * 1024
MXU_WIDTH = 256
TAIL_ORDER = (0, 1, 0, 1, 0, 0, 1, 1, 0, 0, 1, 1, 0, 0, 1, 1)


def _mixer_order(n_chunks):
    order = [0, 0, 0]
    for c in range(n_chunks):
        nxt = [c + 1] if c + 1 < n_chunks else []
        order += nxt + [c, c] + nxt + [c, c] + nxt + [c]
    return order


def _dot(a, b):
    return jnp.dot(a, b, preferred_element_type=F32)


def _pack_rows(w):
    k, n = w.shape
    rows = 256
    assert k % rows == 0

    def pack_kernel(w_ref, o_ref):
        o_ref[...] = pltpu.bitcast(w_ref[...].astype(BF16), jnp.uint32)

    return pl.pallas_call(
        pack_kernel,
        grid=(k // rows,),
        in_specs=[pl.BlockSpec((rows, n), lambda i: (i, 0))],
        out_specs=pl.BlockSpec((rows // 2, n), lambda i: (i, 0)),
        out_shape=jax.ShapeDtypeStruct((k // 2, n), jnp.uint32),
        compiler_params=pltpu.CompilerParams(vmem_limit_bytes=VMEM_LIMIT_BYTES),
        name="pack_weight",
    )(w)


def _w(ref, rows=None, cols=None):
    r0, r1 = rows if rows is not None else (0, None)
    c0, c1 = cols if cols is not None else (0, None)
    packed = ref[r0 // 2:(None if r1 is None else r1 // 2), c0:c1]
    return pltpu.bitcast(packed, BF16)


def _rms(x, w):
    return x * lax.rsqrt(jnp.mean(x * x, axis=-1, keepdims=True) + EPS) * w


def _silu(x):
    return x * (1.0 / (1.0 + jnp.exp(-x)))


def _ffn_blocks(d_ff):
    step = 3 * MXU_WIDTH
    return [(lo, min(lo + step, d_ff)) for lo in range(0, d_ff, step)]


def _run_ordered(stage_iters, order):
    for i in order:
        next(stage_iters[i], None)
    for it in stage_iters:
        assert next(it, "end") == "end", "stage order does not cover every stage"


def _gelu_tanh(x):
    c = math.sqrt(2.0 / math.pi)
    return x * (0.5 * (1.0 + jnp.tanh(c * (x + 0.044715 * (x * x * x)))))


def _rope_kernel(inv_ref, cos_ref, sin_ref, cos_off_ref, sin_off_ref):
    rows = cos_ref.shape[0]
    half = HEAD_DIM // 2

    @pl.when(pl.program_id(0) == 0)
    def _():
        off = lax.broadcasted_iota(jnp.int32, (rows, HEAD_DIM), 0).astype(F32) * inv_ref[...]
        cos_off_ref[...] = jnp.cos(off)
        sin_off_ref[...] = jnp.sin(off)

    base = (pl.program_id(0) * rows).astype(F32) * inv_ref[...]
    cos_b = jnp.cos(base)
    sin_b = jnp.sin(base)
    lane = lax.broadcasted_iota(jnp.int32, (1, HEAD_DIM), 1)
    sign = jnp.where(lane < half, -1.0, 1.0)
    cos_ref[...] = cos_b * cos_off_ref[...] - sin_b * sin_off_ref[...]
    sin_ref[...] = (sign * sin_b) * cos_off_ref[...] + (sign * cos_b) * sin_off_ref[...]


def _rope_tables(seq):
    half = HEAD_DIM // 2
    inv = ROPE_BASE ** (-jnp.arange(half, dtype=F32) / half)
    inv2 = jnp.concatenate([inv, inv]).reshape(1, HEAD_DIM)
    rows = 512
    assert seq % rows == 0
    return pl.pallas_call(
        _rope_kernel,
        grid=(seq // rows,),
        in_specs=[pl.BlockSpec((1, HEAD_DIM), lambda i: (0, 0))],
        out_specs=[pl.BlockSpec((rows, HEAD_DIM), lambda i: (i, 0))] * 2,
        out_shape=[jax.ShapeDtypeStruct((seq, HEAD_DIM), F32)] * 2,
        scratch_shapes=[pltpu.VMEM((rows, HEAD_DIM), F32)] * 2,
        compiler_params=pltpu.CompilerParams(dimension_semantics=("arbitrary",)),
        name="rope_tables",
    )(inv2)


def _decay_kernel(lg_ref, d_ref, tab_ref, dec_ref):
    c = d_ref.shape[1]
    ii = lax.broadcasted_iota(jnp.int32, (c, c), 0)
    jj = lax.broadcasted_iota(jnp.int32, (c, c), 1)
    dist = (ii - jj).astype(F32)
    adist = jnp.abs(dist)
    idx = lax.broadcasted_iota(jnp.int32, (c, HEAD_DIM), 0).astype(F32)
    ones = jnp.ones((1, HEAD_DIM), F32)
    for h in range(RET_HEADS):
        lf = lg_ref[0, h]
        lb = lg_ref[1, h]
        d_ref[h] = jnp.where(dist >= 0, jnp.exp(lf * adist), jnp.exp(lb * adist))
        sl = slice(h * HEAD_DIM, (h + 1) * HEAD_DIM)
        tab_ref[0, :, sl] = jnp.exp(lf * (idx + 1.0))
        tab_ref[1, :, sl] = jnp.exp(lb * (c - idx))
        tab_ref[2, :, sl] = jnp.exp(lf * (c - 1 - idx))
        tab_ref[3, :, sl] = jnp.exp(lb * idx)
        dec_ref[h:h + 1, :] = jnp.exp((lf * c) * ones)
        dec_ref[RET_HEADS + h:RET_HEADS + h + 1, :] = jnp.exp((lb * c) * ones)


def _decay_tables(log_gamma):
    c = RET_CHUNK
    width = RET_HEADS * HEAD_DIM
    return pl.pallas_call(
        _decay_kernel,
        in_specs=[pl.BlockSpec(memory_space=pltpu.SMEM)],
        out_shape=[
            jax.ShapeDtypeStruct((RET_HEADS, c, c), F32),
            jax.ShapeDtypeStruct((4, c, width), F32),
            jax.ShapeDtypeStruct((2 * RET_HEADS, HEAD_DIM), F32),
        ],
        name="decay_tables",
    )(log_gamma.astype(F32))


def _mem_kernel(mem_ref, nw_ref, wkv_ref, kt_ref, v_ref):
    d = mem_ref.shape[2]
    mn = _rms(mem_ref[0], nw_ref[...]).astype(BF16)
    kv = _dot(mn, _w(wkv_ref))
    kt_ref[0] = kv[:, :d].T.astype(BF16)
    v_ref[0] = kv[:, d:].astype(BF16)


def _mem_kv(mem, nw_mem, wkv):
    b, m, d = mem.shape
    return pl.pallas_call(
        _mem_kernel,
        grid=(b,),
        in_specs=[
            pl.BlockSpec((1, m, d), lambda i: (i, 0, 0)),
            pl.BlockSpec((1, d), lambda i: (0, 0)),
            pl.BlockSpec(wkv.shape, lambda i: (0, 0), pipeline_mode=pl.Buffered(1)),
        ],
        out_specs=[
            pl.BlockSpec((1, d, m), lambda i: (i, 0, 0)),
            pl.BlockSpec((1, m, d), lambda i: (i, 0, 0)),
        ],
        out_shape=[
            jax.ShapeDtypeStruct((b, d, m), BF16),
            jax.ShapeDtypeStruct((b, m, d), BF16),
        ],
        compiler_params=pltpu.CompilerParams(
            dimension_semantics=("arbitrary",), vmem_limit_bytes=VMEM_LIMIT_BYTES),
        name="mem_kv",
    )(mem, nw_mem, wkv)


def _mixer_kernel(x_ref, nw_ref, win_ref, cos_ref, sin_ref, d_ref, tab_ref, dec_ref,
                  sgu_nw_ref, sgu_w_ref, sgu_bt_ref,
                  qb_ref, wb_ref, y_ref, sg_ref, sgu_ref, rf_ref):
    rw = RET_HEADS * HEAD_DIM
    sw = SGU_GROUPS * SGU_CHUNK
    k_scale = HEAD_DIM ** -0.5

    @pl.when(pl.program_id(1) == 0)
    def _():
        rf_ref[...] = jnp.zeros_like(rf_ref)

    def chunk_stages(r):
        h = _rms(x_ref[0, r, :], nw_ref[...]).astype(BF16)
        yield
        zr = _dot(h, _w(win_ref, cols=(0, 3 * rw)))
        yield
        zg = _dot(h, _w(win_ref, cols=(3 * rw, None)))
        yield
        cos = cos_ref[r, :]
        sin = sin_ref[r, :]
        qs, ks, vs = [], [], []
        for hd in range(RET_HEADS):
            sl = slice(hd * HEAD_DIM, (hd + 1) * HEAD_DIM)
            q = zr[:, hd * HEAD_DIM:(hd + 1) * HEAD_DIM]
            k = zr[:, rw + hd * HEAD_DIM:rw + (hd + 1) * HEAD_DIM]
            v = zr[:, 2 * rw + hd * HEAD_DIM:2 * rw + (hd + 1) * HEAD_DIM]
            q = q * cos + pltpu.roll(q, HEAD_DIM // 2, 1) * sin
            k = (k * cos + pltpu.roll(k, HEAD_DIM // 2, 1) * sin) * k_scale
            k16 = k.astype(BF16)
            qb_ref[0, r, sl] = (q * tab_ref[1, :, sl]).astype(BF16)
            qs.append(q)
            ks.append(k16)
            vs.append(v)
        yield
        scores, y_state = [], []
        for hd in range(RET_HEADS):
            sl = slice(hd * HEAD_DIM, (hd + 1) * HEAD_DIM)
            q, k16, v = qs[hd], ks[hd], vs[hd]
            qf = (q * tab_ref[0, :, sl]).astype(BF16)
            y_state.append(_dot(qf, rf_ref[hd].astype(BF16)))
            vfb = jnp.concatenate([v * tab_ref[2, :, sl], v * tab_ref[3, :, sl]], axis=1).astype(BF16)
            w = lax.dot_general(k16, vfb, (((0,), (0,)), ((), ())), preferred_element_type=F32)
            rf_ref[hd] = dec_ref[hd:hd + 1, :] * rf_ref[hd] + w[:, :HEAD_DIM]
            wb_ref[0, r.start // RET_CHUNK, sl, :] = w[:, HEAD_DIM:]
            scores.append(lax.dot_general(q.astype(BF16), k16, (((1,), (1,)), ((), ())),
                                          preferred_element_type=F32))
        yield
        sg_ref[0, r, :] = _silu(zg[:, :rw]).astype(BF16)
        u = _gelu_tanh(zg[:, rw:rw + sw])
        vg = _gelu_tanh(zg[:, rw + sw:rw + 2 * sw])
        mu = jnp.mean(vg, axis=-1, keepdims=True)
        vc = vg - mu
        var = jnp.mean(vc * vc, axis=-1, keepdims=True)
        vn = (vc * lax.rsqrt(var + EPS) * sgu_nw_ref[...]).astype(BF16)
        yield
        for hd in range(RET_HEADS):
            sl = slice(hd * HEAD_DIM, (hd + 1) * HEAD_DIM)
            p = (scores[hd] * d_ref[hd]).astype(BF16)
            y_ref[0, r, sl] = _dot(p, vs[hd].astype(BF16)) + y_state[hd]
        yield
        n_sub = RET_CHUNK // SGU_CHUNK
        for g in range(SGU_GROUPS):
            cols = slice(g * SGU_CHUNK, (g + 1) * SGU_CHUNK)
            rhs = jnp.concatenate([vn[c * SGU_CHUNK:(c + 1) * SGU_CHUNK, cols] for c in range(n_sub)], axis=1)
            sp = _dot(sgu_w_ref[g], rhs) + sgu_bt_ref[:, g:g + 1]
            for c in range(n_sub):
                rows = slice(c * SGU_CHUNK, (c + 1) * SGU_CHUNK)
                sgu_ref[0, r.start + c * SGU_CHUNK:r.start + (c + 1) * SGU_CHUNK, cols] = (
                    u[rows, cols] * sp[:, rows]).astype(BF16)

    n_chunks = x_ref.shape[1] // RET_CHUNK
    _run_ordered([chunk_stages(slice(st * RET_CHUNK, (st + 1) * RET_CHUNK)) for st in range(n_chunks)],
                 _mixer_order(n_chunks))


def _mixer(x, nw_pre, w_in, cos_t, sin_t, d_t, tab_t, dec_t, sgu_nw, sgu_w, sgu_bt):
    b, s, d = x.shape
    tm = MIXER_TILE
    assert s % tm == 0 and tm % RET_CHUNK == 0
    rw = RET_HEADS * HEAD_DIM
    const2 = lambda i, j: (0, 0)
    const3 = lambda i, j: (0, 0, 0)
    tok = lambda i, j: (i, j, 0)
    out_tok = pl.BlockSpec((1, tm, rw), tok)
    return pl.pallas_call(
        _mixer_kernel,
        grid=(b, s // tm),
        in_specs=[
            pl.BlockSpec((1, tm, d), tok),
            pl.BlockSpec((1, d), const2),
            pl.BlockSpec(w_in.shape, const2, pipeline_mode=pl.Buffered(1)),
            pl.BlockSpec((tm, HEAD_DIM), lambda i, j: (j, 0)),
            pl.BlockSpec((tm, HEAD_DIM), lambda i, j: (j, 0)),
            pl.BlockSpec(d_t.shape, const3, pipeline_mode=pl.Buffered(1)),
            pl.BlockSpec(tab_t.shape, const3, pipeline_mode=pl.Buffered(1)),
            pl.BlockSpec(dec_t.shape, const2),
            pl.BlockSpec(sgu_nw.shape, const2),
            pl.BlockSpec(sgu_w.shape, const3),
            pl.BlockSpec(sgu_bt.shape, const2),
        ],
        out_specs=[out_tok,
                   pl.BlockSpec((1, tm // RET_CHUNK, rw, HEAD_DIM), lambda i, j: (i, j, 0, 0)),
                   out_tok, out_tok, out_tok],
        out_shape=[
            jax.ShapeDtypeStruct((b, s, rw), BF16),
            jax.ShapeDtypeStruct((b, s // RET_CHUNK, rw, HEAD_DIM), F32),
            jax.ShapeDtypeStruct((b, s, rw), F32),
            jax.ShapeDtypeStruct((b, s, rw), BF16),
            jax.ShapeDtypeStruct((b, s, rw), BF16),
        ],
        scratch_shapes=[pltpu.VMEM((RET_HEADS, HEAD_DIM, HEAD_DIM), F32)],
        compiler_params=pltpu.CompilerParams(
            dimension_semantics=("arbitrary", "arbitrary"), vmem_limit_bytes=VMEM_LIMIT_BYTES),
        name="mixer",
    )(x, nw_pre, w_in, cos_t, sin_t, d_t, tab_t, dec_t, sgu_nw, sgu_w, sgu_bt)


def _tail_kernel(x_ref, qb_ref, wb_ref, y_ref, sg_ref, sgu_ref, dec_ref, gnw_ref,
                 nw_ref, wout_ref, wq_ref, kt_ref, vm_ref, wo_ref, wgu_ref, wdown_ref,
                 o_ref, rb_ref):
    d_ff = wdown_ref.shape[0] * 2
    xa_d = wq_ref.shape[1] // XA_HEADS

    @pl.when(pl.program_id(1) == 0)
    def _():
        rb_ref[...] = jnp.zeros_like(rb_ref)

    def chunk_stages(r):
        rw = RET_HEADS * HEAD_DIM
        ret = []
        for hd in range(RET_HEADS):
            sl = slice(hd * HEAD_DIM, (hd + 1) * HEAD_DIM)
            y = y_ref[0, r, sl] + _dot(qb_ref[0, r, sl], rb_ref[hd].astype(BF16))
            rb_ref[hd] = (dec_ref[RET_HEADS + hd:RET_HEADS + hd + 1, :] * rb_ref[hd]
                          + wb_ref[0, r.start // RET_CHUNK, sl, :])
            mu = jnp.mean(y, axis=-1, keepdims=True)
            yc = y - mu
            var = jnp.mean(yc * yc, axis=-1, keepdims=True)
            yn = yc * lax.rsqrt(var + EPS) * gnw_ref[:, sl]
            ret.append((sg_ref[0, r, sl].astype(F32) * yn).astype(BF16))
        yield
        mix_sgu = _dot(sgu_ref[0, r, :], _w(wout_ref, rows=(rw, None)))
        mix = _dot(jnp.concatenate(ret, axis=-1), _w(wout_ref, rows=(0, rw))) + mix_sgu
        yield
        x = x_ref[0, r, :] + _rms(mix, nw_ref[0:1, :])
        hq = _rms(x, nw_ref[1:2, :]).astype(BF16)
        yield
        q = _dot(hq, _w(wq_ref)) * (xa_d ** -0.5)
        head_cols = [slice(hd * xa_d, (hd + 1) * xa_d) for hd in range(XA_HEADS)]
        scores = [_dot(q[:, sl].astype(BF16), kt_ref[0, sl, :]) for sl in head_cols]
        yield
        heads = []
        for sl, sc in zip(head_cols, scores):
            e = jnp.exp(sc - jnp.max(sc, axis=-1, keepdims=True))
            p = (e * (1.0 / jnp.sum(e, axis=-1, keepdims=True))).astype(BF16)
            heads.append(_dot(p, vm_ref[0, :, sl]).astype(BF16))
        yield
        xa = _dot(jnp.concatenate(heads, axis=-1), _w(wo_ref))
        yield
        x = x + _rms(xa, nw_ref[2:3, :])
        hf = _rms(x, nw_ref[3:4, :]).astype(BF16)
        yield
        act0 = swiglu_block(hf, *blocks[0])
        carried.append((r, x, hf, act0))

    def swiglu_block(hf, lo, hi):
        g = _dot(hf, _w(wgu_ref, cols=(lo, hi)))
        u = _dot(hf, _w(wgu_ref, cols=(d_ff + lo, d_ff + hi)))
        return (_silu(g) * u).astype(BF16)

    blocks = _ffn_blocks(d_ff)
    carried = []
    n_chunks = x_ref.shape[1] // RET_CHUNK
    _run_ordered([chunk_stages(slice(st * RET_CHUNK, (st + 1) * RET_CHUNK))
                  for st in reversed(range(n_chunks))], TAIL_ORDER)

    hf_all = jnp.concatenate([c[2] for c in carried], axis=0)
    rest = [swiglu_block(hf_all, lo, hi) for lo, hi in blocks[1:]]
    ffs = []
    for i, (r, x, _, act0) in enumerate(carried):
        rows = slice(i * RET_CHUNK, (i + 1) * RET_CHUNK)
        act = jnp.concatenate([act0] + [a[rows, :] for a in rest], axis=-1)
        ffs.append(_dot(act, _w(wdown_ref)))
    for (r, x, _, _), ff in zip(carried, ffs):
        o_ref[0, r, :] = x + _rms(ff, nw_ref[4:5, :])


def _tail(x, qb, wb, y, sg, sgu, dec_t, gn_w, nw_tail, w_out, wq, kt, vm, wo, w_gu, w_down):
    b, s, d = x.shape
    tm = TOKEN_TILE
    nt = s // tm
    rw = RET_HEADS * HEAD_DIM
    n_mem = vm.shape[1]
    const2 = lambda i, j: (0, 0)
    rev = lambda i, j: (i, nt - 1 - j, 0)
    per_b = lambda i, j: (i, 0, 0)
    tok_d = pl.BlockSpec((1, tm, d), rev)
    tok_r = pl.BlockSpec((1, tm, rw), rev)
    weight = lambda w: pl.BlockSpec(w.shape, const2, pipeline_mode=pl.Buffered(1))
    return pl.pallas_call(
        _tail_kernel,
        grid=(b, nt),
        in_specs=[
            tok_d, tok_r,
            pl.BlockSpec((1, tm // RET_CHUNK, rw, HEAD_DIM), lambda i, j: (i, nt - 1 - j, 0, 0)),
            tok_r, tok_r, tok_r,
            pl.BlockSpec(dec_t.shape, const2),
            pl.BlockSpec(gn_w.shape, const2),
            pl.BlockSpec(nw_tail.shape, const2),
            weight(w_out), weight(wq),
            pl.BlockSpec((1, d, n_mem), per_b),
            pl.BlockSpec((1, n_mem, d), per_b),
            weight(wo), weight(w_gu), weight(w_down),
        ],
        out_specs=tok_d,
        out_shape=jax.ShapeDtypeStruct((b, s, d), F32),
        scratch_shapes=[pltpu.VMEM((RET_HEADS, HEAD_DIM, HEAD_DIM), F32)],
        compiler_params=pltpu.CompilerParams(
            dimension_semantics=("arbitrary", "arbitrary"), vmem_limit_bytes=VMEM_LIMIT_BYTES),
        name="tail",
    )(x, qb, wb, y, sg, sgu, dec_t, gn_w, nw_tail, w_out, wq, kt, vm, wo, w_gu, w_down)


def _encoder_layer(x, mem, tables, nw, w_in, gn_w, sgu_nw, sgu_w, sgu_bt, w_out, wq, wkv, wo, w_gu, w_down):
    cos_t, sin_t, d_t, tab_t, dec_t = tables
    row = lambda i: nw[i:i + 1, :]
    kt, vm = _mem_kv(mem, row(4), wkv)
    qb, wb, y, sg, sgu = _mixer(x, row(0), w_in, cos_t, sin_t, d_t, tab_t, dec_t, sgu_nw, sgu_w, sgu_bt)
    nw_tail = jnp.concatenate([row(1), row(2), row(3), row(5), row(6)], axis=0)
    return _tail(x, qb, wb, y, sg, sgu, dec_t, gn_w, nw_tail, w_out, wq, kt, vm, wo, w_gu, w_down)


def kernel(x_prompt, x_sample, mem_prompt, mem_sample, norm_w, w_in, ret_log_gamma, ret_gn_w, sgu_norm_w,
           sgu_w, sgu_b, w_out, xa_wq, xa_wkv, xa_wo, ffn_w_gu, ffn_w_down):
    depth = norm_w.shape[0]
    assert w_in.shape[2] == 4 * RET_HEADS * HEAD_DIM + 2 * SGU_GROUPS * SGU_CHUNK
    seq = max(x_prompt.shape[1], x_sample.shape[1])
    cos_t, sin_t = _rope_tables(seq)
    y_prompt, y_sample = x_prompt, x_sample
    for l in range(depth):
        d_t, tab_t, dec_t = _decay_tables(ret_log_gamma[l])
        tables = (cos_t, sin_t, d_t, tab_t, dec_t)
        args = (norm_w[l], _pack_rows(w_in[l]), ret_gn_w[l].reshape(1, -1), sgu_norm_w[l].reshape(1, -1),
                sgu_w[l].astype(BF16), jnp.transpose(sgu_b[l]), _pack_rows(w_out[l]), _pack_rows(xa_wq[l]),
                _pack_rows(xa_wkv[l]), _pack_rows(xa_wo[l]), _pack_rows(ffn_w_gu[l]),
                _pack_rows(ffn_w_down[l]))
        y_prompt = _encoder_layer(y_prompt, mem_prompt, tables, *args)
        y_sample = _encoder_layer(y_sample, mem_sample, tables, *args)
    return (y_prompt, y_sample)
```

```python
import math

import jax
import jax.numpy as jnp
from jax import lax
from jax.experimental import pallas as pl
from jax.experimental.pallas import tpu as pltpu

F32 = jnp.float32
BF16 = jnp.bfloat16

EPS = 1e-6
ROPE_BASE = 10000.0
RET_HEADS = 4
HEAD_DIM = 128
SGU_GROUPS = 4
SGU_CHUNK = 128
XA_HEADS = 4
RET_CHUNK = 256
TOKEN_TILE = 512
MIXER_TILE = 1024
VMEM_LIMIT_BYTES = 60 * 1024 * 1024
MXU_WIDTH = 256
TAIL_ORDER = (0, 1, 0, 1, 0, 0, 1, 1, 0, 0, 1, 1, 0, 0, 1, 1)


def _mixer_order(n_chunks):
    order = [0, 0, 0]
    for c in range(n_chunks):
        nxt = [c + 1] if c + 1 < n_chunks else []
        order += nxt + [c, c] + nxt + [c, c] + nxt + [c]
    return order


def _dot(a, b):
    return jnp.dot(a, b, preferred_element_type=F32)


def _pack_rows(w):
    k, n = w.shape
    rows = 256
    assert k % rows == 0

    def pack_kernel(w_ref, o_ref):
        o_ref[...] = pltpu.bitcast(w_ref[...].astype(BF16), jnp.uint32)

    return pl.pallas_call(
        pack_kernel,
        grid=(k // rows,),
        in_specs=[pl.BlockSpec((rows, n), lambda i: (i, 0))],
        out_specs=pl.BlockSpec((rows // 2, n), lambda i: (i, 0)),
        out_shape=jax.ShapeDtypeStruct((k // 2, n), jnp.uint32),
        compiler_params=pltpu.CompilerParams(vmem_limit_bytes=VMEM_LIMIT_BYTES),
        name="pack_weight",
    )(w)


def _w(ref, rows=None, cols=None):
    r0, r1 = rows if rows is not None else (0, None)
    c0, c1 = cols if cols is not None else (0, None)
    packed = ref[r0 // 2:(None if r1 is None else r1 // 2), c0:c1]
    return pltpu.bitcast(packed, BF16)


def _rms(x, w):
    return x * lax.rsqrt(jnp.mean(x * x, axis=-1, keepdims=True) + EPS) * w


def _silu(x):
    return x * (1.0 / (1.0 + jnp.exp(-x)))


def FFN_BLOCKS(d_ff):
    step = 3 * MXU_WIDTH
    return [(lo, min(lo + step, d_ff)) for lo in range(0, d_ff, step)]


def _run_ordered(stage_iters, order):
    for i in order:
        next(stage_iters[i], None)
    for it in stage_iters:
        assert next(it, "end") == "end", "stage order does not cover every stage"


def _gelu_tanh(x):
    c = math.sqrt(2.0 / math.pi)
    return x * (0.5 * (1.0 + jnp.tanh(c * (x + 0.044715 * (x * x * x)))))


def _rope_kernel(inv_ref, cos_ref, sin_ref, cos_off_ref, sin_off_ref):
    rows = cos_ref.shape[0]
    half = HEAD_DIM // 2

    @pl.when(pl.program_id(0) == 0)
    def _():
        off = lax.broadcasted_iota(jnp.int32, (rows, HEAD_DIM), 0).astype(F32) * inv_ref[...]
        cos_off_ref[...] = jnp.cos(off)
        sin_off_ref[...] = jnp.sin(off)

    base = (pl.program_id(0) * rows).astype(F32) * inv_ref[...]
    cos_b = jnp.cos(base)
    sin_b = jnp.sin(base)
    lane = lax.broadcasted_iota(jnp.int32, (1, HEAD_DIM), 1)
    sign = jnp.where(lane < half, -1.0, 1.0)
    cos_ref[...] = cos_b * cos_off_ref[...] - sin_b * sin_off_ref[...]
    sin_ref[...] = (sign * sin_b) * cos_off_ref[...] + (sign * cos_b) * sin_off_ref[...]


def _rope_tables(seq):
    half = HEAD_DIM // 2
    inv = ROPE_BASE ** (-jnp.arange(half, dtype=F32) / half)
    inv2 = jnp.concatenate([inv, inv]).reshape(1, HEAD_DIM)
    rows = 512
    assert seq % rows == 0
    return pl.pallas_call(
        _rope_kernel,
        grid=(seq // rows,),
        in_specs=[pl.BlockSpec((1, HEAD_DIM), lambda i: (0, 0))],
        out_specs=[pl.BlockSpec((rows, HEAD_DIM), lambda i: (i, 0))] * 2,
        out_shape=[jax.ShapeDtypeStruct((seq, HEAD_DIM), F32)] * 2,
        scratch_shapes=[pltpu.VMEM((rows, HEAD_DIM), F32)] * 2,
        compiler_params=pltpu.CompilerParams(dimension_semantics=("arbitrary",)),
        name="rope_tables",
    )(inv2)


def _decay_kernel(lg_ref, d_ref, tab_ref, dec_ref):
    c = d_ref.shape[1]
    ii = lax.broadcasted_iota(jnp.int32, (c, c), 0)
    jj = lax.broadcasted_iota(jnp.int32, (c, c), 1)
    dist = (ii - jj).astype(F32)
    adist = jnp.abs(dist)
    idx = lax.broadcasted_iota(jnp.int32, (c, HEAD_DIM), 0).astype(F32)
    ones = jnp.ones((1, HEAD_DIM), F32)
    for h in range(RET_HEADS):
        lf = lg_ref[0, h]
        lb = lg_ref[1, h]
        d_ref[h] = jnp.where(dist >= 0, jnp.exp(lf * adist), jnp.exp(lb * adist))
        sl = slice(h * HEAD_DIM, (h + 1) * HEAD_DIM)
        tab_ref[0, :, sl] = jnp.exp(lf * (idx + 1.0))
        tab_ref[1, :, sl] = jnp.exp(lb * (c - idx))
        tab_ref[2, :, sl] = jnp.exp(lf * (c - 1 - idx))
        tab_ref[3, :, sl] = jnp.exp(lb * idx)
        dec_ref[h:h + 1, :] = jnp.exp((lf * c) * ones)
        dec_ref[RET_HEADS + h:RET_HEADS + h + 1, :] = jnp.exp((lb * c) * ones)


def _decay_tables(log_gamma):
    c = RET_CHUNK
    width = RET_HEADS * HEAD_DIM
    return pl.pallas_call(
        _decay_kernel,
        in_specs=[pl.BlockSpec(memory_space=pltpu.SMEM)],
        out_shape=[
            jax.ShapeDtypeStruct((RET_HEADS, c, c), F32),
            jax.ShapeDtypeStruct((4, c, width), F32),
            jax.ShapeDtypeStruct((2 * RET_HEADS, HEAD_DIM), F32),
        ],
        name="decay_tables",
    )(log_gamma.astype(F32))


def _mem_kernel(mem_ref, nw_ref, wkv_ref, kt_ref, v_ref):
    d = mem_ref.shape[2]
    mn = _rms(mem_ref[0], nw_ref[...]).astype(BF16)
    kv = _dot(mn, _w(wkv_ref))
    kt_ref[0] = kv[:, :d].T.astype(BF16)
    v_ref[0] = kv[:, d:].astype(BF16)


def _mem_kv(mem, nw_mem, wkv):
    b, m, d = mem.shape
    return pl.pallas_call(
        _mem_kernel,
        grid=(b,),
        in_specs=[
            pl.BlockSpec((1, m, d), lambda i: (i, 0, 0)),
            pl.BlockSpec((1, d), lambda i: (0, 0)),
            pl.BlockSpec(wkv.shape, lambda i: (0, 0), pipeline_mode=pl.Buffered(1)),
        ],
        out_specs=[
            pl.BlockSpec((1, d, m), lambda i: (i, 0, 0)),
            pl.BlockSpec((1, m, d), lambda i: (i, 0, 0)),
        ],
        out_shape=[
            jax.ShapeDtypeStruct((b, d, m), BF16),
            jax.ShapeDtypeStruct((b, m, d), BF16),
        ],
        compiler_params=pltpu.CompilerParams(
            dimension_semantics=("arbitrary",), vmem_limit_bytes=VMEM_LIMIT_BYTES),
        name="mem_kv",
    )(mem, nw_mem, wkv)


def _mixer_kernel(x_ref, nw_ref, win_ref, cos_ref, sin_ref, d_ref, tab_ref, dec_ref,
                  sgu_nw_ref, sgu_w_ref, sgu_bt_ref,
                  qb_ref, wb_ref, y_ref, sg_ref, sgu_ref, rf_ref):
    rw = RET_HEADS * HEAD_DIM
    sw = SGU_GROUPS * SGU_CHUNK
    k_scale = HEAD_DIM ** -0.5

    @pl.when(pl.program_id(1) == 0)
    def _():
        rf_ref[...] = jnp.zeros_like(rf_ref)

    def chunk_stages(r):
        h = _rms(x_ref[0, r, :], nw_ref[...]).astype(BF16)
        yield
        zr = _dot(h, _w(win_ref, cols=(0, 3 * rw)))
        yield
        zg = _dot(h, _w(win_ref, cols=(3 * rw, None)))
        yield
        cos = cos_ref[r, :]
        sin = sin_ref[r, :]
        qs, ks, vs = [], [], []
        for hd in range(RET_HEADS):
            sl = slice(hd * HEAD_DIM, (hd + 1) * HEAD_DIM)
            q = zr[:, hd * HEAD_DIM:(hd + 1) * HEAD_DIM]
            k = zr[:, rw + hd * HEAD_DIM:rw + (hd + 1) * HEAD_DIM]
            v = zr[:, 2 * rw + hd * HEAD_DIM:2 * rw + (hd + 1) * HEAD_DIM]
            q = q * cos + pltpu.roll(q, HEAD_DIM // 2, 1) * sin
            k = (k * cos + pltpu.roll(k, HEAD_DIM // 2, 1) * sin) * k_scale
            k16 = k.astype(BF16)
            qb_ref[0, r, sl] = (q * tab_ref[1, :, sl]).astype(BF16)
            qs.append(q)
            ks.append(k16)
            vs.append(v)
        yield
        scores, y_state = [], []
        for hd in range(RET_HEADS):
            sl = slice(hd * HEAD_DIM, (hd + 1) * HEAD_DIM)
            q, k16, v = qs[hd], ks[hd], vs[hd]
            qf = (q * tab_ref[0, :, sl]).astype(BF16)
            y_state.append(_dot(qf, rf_ref[hd].astype(BF16)))
            vfb = jnp.concatenate([v * tab_ref[2, :, sl], v * tab_ref[3, :, sl]], axis=1).astype(BF16)
            w = lax.dot_general(k16, vfb, (((0,), (0,)), ((), ())), preferred_element_type=F32)
            rf_ref[hd] = dec_ref[hd:hd + 1, :] * rf_ref[hd] + w[:, :HEAD_DIM]
            wb_ref[0, r.start // RET_CHUNK, sl, :] = w[:, HEAD_DIM:]
            scores.append(lax.dot_general(q.astype(BF16), k16, (((1,), (1,)), ((), ())),
                                          preferred_element_type=F32))
        yield
        sg_ref[0, r, :] = _silu(zg[:, :rw]).astype(BF16)
        u = _gelu_tanh(zg[:, rw:rw + sw])
        vg = _gelu_tanh(zg[:, rw + sw:rw + 2 * sw])
        mu = jnp.mean(vg, axis=-1, keepdims=True)
        vc = vg - mu
        var = jnp.mean(vc * vc, axis=-1, keepdims=True)
        vn = (vc * lax.rsqrt(var + EPS) * sgu_nw_ref[...]).astype(BF16)
        yield
        for hd in range(RET_HEADS):
            sl = slice(hd * HEAD_DIM, (hd + 1) * HEAD_DIM)
            p = (scores[hd] * d_ref[hd]).astype(BF16)
            y_ref[0, r, sl] = (_dot(p, vs[hd].astype(BF16)) + y_state[hd]).astype(BF16)
        yield
        n_sub = RET_CHUNK // SGU_CHUNK
        for g in range(SGU_GROUPS):
            cols = slice(g * SGU_CHUNK, (g + 1) * SGU_CHUNK)
            rhs = jnp.concatenate([vn[c * SGU_CHUNK:(c + 1) * SGU_CHUNK, cols] for c in range(n_sub)], axis=1)
            sp = _dot(sgu_w_ref[g], rhs) + sgu_bt_ref[:, g:g + 1]
            for c in range(n_sub):
                rows = slice(c * SGU_CHUNK, (c + 1) * SGU_CHUNK)
                sgu_ref[0, r.start + c * SGU_CHUNK:r.start + (c + 1) * SGU_CHUNK, cols] = (
                    u[rows, cols] * sp[:, rows]).astype(BF16)

    n_chunks = x_ref.shape[1] // RET_CHUNK
    _run_ordered([chunk_stages(slice(st * RET_CHUNK, (st + 1) * RET_CHUNK)) for st in range(n_chunks)],
                 _mixer_order(n_chunks))


def _mixer(x, nw_pre, w_in, cos_t, sin_t, d_t, tab_t, dec_t, sgu_nw, sgu_w, sgu_bt):
    b, s, d = x.shape
    tm = MIXER_TILE
    assert s % tm == 0 and tm % RET_CHUNK == 0
    rw = RET_HEADS * HEAD_DIM
    const2 = lambda i, j: (0, 0)
    const3 = lambda i, j: (0, 0, 0)
    tok = lambda i, j: (i, j, 0)
    out_tok = pl.BlockSpec((1, tm, rw), tok)
    return pl.pallas_call(
        _mixer_kernel,
        grid=(b, s // tm),
        in_specs=[
            pl.BlockSpec((1, tm, d), tok),
            pl.BlockSpec((1, d), const2),
            pl.BlockSpec(w_in.shape, const2, pipeline_mode=pl.Buffered(1)),
            pl.BlockSpec((tm, HEAD_DIM), lambda i, j: (j, 0)),
            pl.BlockSpec((tm, HEAD_DIM), lambda i, j: (j, 0)),
            pl.BlockSpec(d_t.shape, const3, pipeline_mode=pl.Buffered(1)),
            pl.BlockSpec(tab_t.shape, const3, pipeline_mode=pl.Buffered(1)),
            pl.BlockSpec(dec_t.shape, const2),
            pl.BlockSpec(sgu_nw.shape, const2),
            pl.BlockSpec(sgu_w.shape, const3),
            pl.BlockSpec(sgu_bt.shape, const2),
        ],
        out_specs=[out_tok,
                   pl.BlockSpec((1, tm // RET_CHUNK, rw, HEAD_DIM), lambda i, j: (i, j, 0, 0)),
                   out_tok, out_tok, out_tok],
        out_shape=[
            jax.ShapeDtypeStruct((b, s, rw), BF16),
            jax.ShapeDtypeStruct((b, s // RET_CHUNK, rw, HEAD_DIM), F32),
            jax.ShapeDtypeStruct((b, s, rw), BF16),
            jax.ShapeDtypeStruct((b, s, rw), BF16),
            jax.ShapeDtypeStruct((b, s, rw), BF16),
        ],
        scratch_shapes=[pltpu.VMEM((RET_HEADS, HEAD_DIM, HEAD_DIM), F32)],
        compiler_params=pltpu.CompilerParams(
            dimension_semantics=("arbitrary", "arbitrary"), vmem_limit_bytes=VMEM_LIMIT_BYTES),
        name="mixer",
    )(x, nw_pre, w_in, cos_t, sin_t, d_t, tab_t, dec_t, sgu_nw, sgu_w, sgu_bt)


def _tail_kernel(x_ref, qb_ref, wb_ref, y_ref, sg_ref, sgu_ref, dec_ref, gnw_ref,
                 nw_ref, wout_ref, wq_ref, kt_ref, vm_ref, wo_ref, wgu_ref, wdown_ref,
                 o_ref, rb_ref):
    d_ff = wdown_ref.shape[0] * 2
    xa_d = wq_ref.shape[1] // XA_HEADS

    @pl.when(pl.program_id(1) == 0)
    def _():
        rb_ref[...] = jnp.zeros_like(rb_ref)

    def chunk_stages(r):
        rw = RET_HEADS * HEAD_DIM
        ret = []
        for hd in range(RET_HEADS):
            sl = slice(hd * HEAD_DIM, (hd + 1) * HEAD_DIM)
            y = y_ref[0, r, sl].astype(F32) + _dot(qb_ref[0, r, sl], rb_ref[hd].astype(BF16))
            rb_ref[hd] = (dec_ref[RET_HEADS + hd:RET_HEADS + hd + 1, :] * rb_ref[hd]
                          + wb_ref[0, r.start // RET_CHUNK, sl, :])
            mu = jnp.mean(y, axis=-1, keepdims=True)
            yc = y - mu
            var = jnp.mean(yc * yc, axis=-1, keepdims=True)
            yn = yc * lax.rsqrt(var + EPS) * gnw_ref[:, sl]
            ret.append((sg_ref[0, r, sl].astype(F32) * yn).astype(BF16))
        yield
        mix_sgu = _dot(sgu_ref[0, r, :], _w(wout_ref, rows=(rw, None)))
        mix = _dot(jnp.concatenate(ret, axis=-1), _w(wout_ref, rows=(0, rw))) + mix_sgu
        yield
        x = x_ref[0, r, :] + _rms(mix, nw_ref[0:1, :])
        hq = _rms(x, nw_ref[1:2, :]).astype(BF16)
        yield
        q = _dot(hq, _w(wq_ref)) * (xa_d ** -0.5)
        head_cols = [slice(hd * xa_d, (hd + 1) * xa_d) for hd in range(XA_HEADS)]
        scores = [_dot(q[:, sl].astype(BF16), kt_ref[0, sl, :]) for sl in head_cols]
        yield
        heads = []
        for sl, sc in zip(head_cols, scores):
            e = jnp.exp(sc - jnp.max(sc, axis=-1, keepdims=True))
            p = (e * (1.0 / jnp.sum(e, axis=-1, keepdims=True))).astype(BF16)
            heads.append(_dot(p, vm_ref[0, :, sl]).astype(BF16))
        yield
        xa = _dot(jnp.concatenate(heads, axis=-1), _w(wo_ref))
        yield
        x = x + _rms(xa, nw_ref[2:3, :])
        hf = _rms(x, nw_ref[3:4, :]).astype(BF16)
        yield
        act0 = swiglu_block(hf, *blocks[0])
        carried.append((r, x, hf, act0))

    def swiglu_block(hf, lo, hi):
        g = _dot(hf, _w(wgu_ref, cols=(lo, hi)))
        u = _dot(hf, _w(wgu_ref, cols=(d_ff + lo, d_ff + hi)))
        return (_silu(g) * u).astype(BF16)

    blocks = FFN_BLOCKS(d_ff)
    carried = []
    n_chunks = x_ref.shape[1] // RET_CHUNK
    _run_ordered([chunk_stages(slice(st * RET_CHUNK, (st + 1) * RET_CHUNK))
                  for st in reversed(range(n_chunks))], TAIL_ORDER)

    hf_all = jnp.concatenate([c[2] for c in carried], axis=0)
    rest = [swiglu_block(hf_all, lo, hi) for lo, hi in blocks[1:]]
    ffs = []
    for i, (r, x, _, act0) in enumerate(carried):
        rows = slice(i * RET_CHUNK, (i + 1) * RET_CHUNK)
        act = jnp.concatenate([act0] + [a[rows, :] for a in rest], axis=-1)
        ffs.append(_dot(act, _w(wdown_ref)))
    for (r, x, _, _), ff in zip(carried, ffs):
        o_ref[0, r, :] = x + _rms(ff, nw_ref[4:5, :])


def _tail(x, qb, wb, y, sg, sgu, dec_t, gn_w, nw_tail, w_out, wq, kt, vm, wo, w_gu, w_down):
    b, s, d = x.shape
    tm = TOKEN_TILE
    nt = s // tm
    rw = RET_HEADS * HEAD_DIM
    n_mem = vm.shape[1]
    const2 = lambda i, j: (0, 0)
    rev = lambda i, j: (i, nt - 1 - j, 0)
    per_b = lambda i, j: (i, 0, 0)
    tok_d = pl.BlockSpec((1, tm, d), rev)
    tok_r = pl.BlockSpec((1, tm, rw), rev)
    weight = lambda w: pl.BlockSpec(w.shape, const2, pipeline_mode=pl.Buffered(1))
    return pl.pallas_call(
        _tail_kernel,
        grid=(b, nt),
        in_specs=[
            tok_d, tok_r,
            pl.BlockSpec((1, tm // RET_CHUNK, rw, HEAD_DIM), lambda i, j: (i, nt - 1 - j, 0, 0)),
            tok_r, tok_r, tok_r,
            pl.BlockSpec(dec_t.shape, const2),
            pl.BlockSpec(gn_w.shape, const2),
            pl.BlockSpec(nw_tail.shape, const2),
            weight(w_out), weight(wq),
            pl.BlockSpec((1, d, n_mem), per_b),
            pl.BlockSpec((1, n_mem, d), per_b),
            weight(wo), weight(w_gu), weight(w_down),
        ],
        out_specs=tok_d,
        out_shape=jax.ShapeDtypeStruct((b, s, d), F32),
        scratch_shapes=[pltpu.VMEM((RET_HEADS, HEAD_DIM, HEAD_DIM), F32)],
        compiler_params=pltpu.CompilerParams(
            dimension_semantics=("arbitrary", "arbitrary"), vmem_limit_bytes=VMEM_LIMIT_BYTES),
        name="tail",
    )(x, qb, wb, y, sg, sgu, dec_t, gn_w, nw_tail, w_out, wq, kt, vm, wo, w_gu, w_down)


def _encoder_layer(x, mem, tables, nw, w_in, gn_w, sgu_nw, sgu_w, sgu_bt, w_out, wq, wkv, wo, w_gu, w_down):
    cos_t, sin_t, d_t, tab_t, dec_t = tables
    row = lambda i: nw[i:i + 1, :]
    kt, vm = _mem_kv(mem, row(4), wkv)
    qb, wb, y, sg, sgu = _mixer(x, row(0), w_in, cos_t, sin_t, d_t, tab_t, dec_t, sgu_nw, sgu_w, sgu_bt)
    nw_tail = jnp.concatenate([row(1), row(2), row(3), row(5), row(6)], axis=0)
    return _tail(x, qb, wb, y, sg, sgu, dec_t, gn_w, nw_tail, w_out, wq, kt, vm, wo, w_gu, w_down)


def kernel(x_prompt, x_sample, mem_prompt, mem_sample, norm_w, w_in, ret_log_gamma, ret_gn_w, sgu_norm_w,
           sgu_w, sgu_b, w_out, xa_wq, xa_wkv, xa_wo, ffn_w_gu, ffn_w_down):
    depth = norm_w.shape[0]
    assert w_in.shape[2] == 4 * RET_HEADS * HEAD_DIM + 2 * SGU_GROUPS * SGU_CHUNK
    seq = max(x_prompt.shape[1], x_sample.shape[1])
    cos_t, sin_t = _rope_tables(seq)
    y_prompt, y_sample = x_prompt, x_sample
    for l in range(depth):
        d_t, tab_t, dec_t = _decay_tables(ret_log_gamma[l])
        tables = (cos_t, sin_t, d_t, tab_t, dec_t)
        args = (norm_w[l], _pack_rows(w_in[l]), ret_gn_w[l].reshape(1, -1), sgu_norm_w[l].reshape(1, -1),
                sgu_w[l].astype(BF16), jnp.transpose(sgu_b[l]), _pack_rows(w_out[l]), _pack_rows(xa_wq[l]),
                _pack_rows(xa_wkv[l]), _pack_rows(xa_wo[l]), _pack_rows(ffn_w_gu[l]),
                _pack_rows(ffn_w_down[l]))
        y_prompt = _encoder_layer(y_prompt, mem_prompt, tables, *args)
        y_sample = _encoder_layer(y_sample, mem_sample, tables, *args)
    return (y_prompt, y_sample)
```
